```python
import math, functools
import jax, jax.numpy as jnp
from jax import lax
import numpy as np

D_MODEL = 1024
BATCH = 4
SEQ = 8192
DEPTH = 2
DEC_BATCH = 32
DEC_SEQ = 1
PAST_LEN = 16384
PAGE_SIZE = 128

HEAD_DIM = 64
MIX_W = D_MODEL
GM_W = MIX_W // 4
GM_HEADS = GM_W // HEAD_DIM
CHUNK = 128
NSA_W = MIX_W // 2
NSA_HEADS = NSA_W // HEAD_DIM
NSA_KV_HEADS = NSA_HEADS // 4
NSA_REP = NSA_HEADS // NSA_KV_HEADS
CMP_BLOCK = 32
SLC_BLOCK = 64
BLK_RATIO = SLC_BLOCK // CMP_BLOCK
TOP_N = 16
WINDOW = 512
Q_BLOCK = 128
RW_W = MIX_W // 4
RW_HEADS = RW_W // HEAD_DIM
W_LORA = 64
A_LORA = 64
G_LORA = 128
RW_COLS = 3 * RW_W + W_LORA + A_LORA + G_LORA
RW_LN_EPS = 64e-5
N_MEM = 256
MEM_HEADS = 4
MEM_HEAD_DIM = D_MODEL // MEM_HEADS
MEM_W = MEM_HEADS * MEM_HEAD_DIM
D_FF = ((8 * D_MODEL // 3) + 127) // 128 * 128
CONV_W = 3
ROPE_THETA = 10000.0
NORM_EPS = 1e-6
LN_EPS = 1e-5
NEG = -1e30
FORCE = 1e4
OFF_GM = 0
OFF_Q = OFF_GM + 2 * GM_W
OFF_KV = OFF_Q + NSA_W
OFF_GATE = OFF_KV + 3 * 2 * NSA_KV_HEADS * HEAD_DIM
OFF_RW = OFF_GATE + 3 * NSA_HEADS
IN_COLS = OFF_RW + RW_COLS

kernel_name = "hymba_gmlp_nsa_rwkv7_step"


def rms_norm(x, g):
    xf = x.astype(jnp.float32)
    y = xf * lax.rsqrt(jnp.mean(xf * xf, axis=-1, keepdims=True) + NORM_EPS)
    return (y * g.astype(jnp.float32)).astype(x.dtype)


def layer_norm(x, g, b):
    xf = x.astype(jnp.float32)
    m = jnp.mean(xf, axis=-1, keepdims=True)
    var = jnp.mean(jnp.square(xf - m), axis=-1, keepdims=True)
    y = (xf - m) * lax.rsqrt(var + LN_EPS) * g.astype(jnp.float32) + b.astype(jnp.float32)
    return y.astype(x.dtype)


def rope(x, pos):
    half = HEAD_DIM // 2
    inv = ROPE_THETA ** (-jnp.arange(half, dtype=jnp.float32) / half)
    ang = pos.astype(jnp.float32)[:, None] * inv[None, :]
    shp = (pos.shape[0],) + (1,) * (x.ndim - 3) + (half,)
    cos, sin = jnp.cos(ang).reshape(shp), jnp.sin(ang).reshape(shp)
    xf = x.astype(jnp.float32)
    x1, x2 = xf[..., :half], xf[..., half:]
    return jnp.concatenate([x1 * cos - x2 * sin, x2 * cos + x1 * sin], axis=-1).astype(x.dtype)


def compress_blocks(kv, pe, w):
    b, t = kv.shape[:2]
    blk = kv.reshape(b, t // CMP_BLOCK, CMP_BLOCK, 2, NSA_KV_HEADS, HEAD_DIM) + pe[None, None, :, :, None, :]
    return jnp.einsum('bclsgd,lsde->bcsge', blk, w)


def nsa_attend(q, qpos, kvc, gather_sel, ns, kvw, wpos, gates):
    nc = kvc.shape[1]
    qs = q * jnp.asarray(HEAD_DIM ** -0.5, q.dtype)
    s_c = jnp.einsum('btgrd,bcgd->btgrc', qs, kvc[:, :, 0]).astype(jnp.float32)
    c_end = jnp.arange(nc) * CMP_BLOCK + (CMP_BLOCK - 1)
    m_c = (c_end[None, :] <= qpos[:, None])[None, :, None, None, :]
    p_c = jax.nn.softmax(jnp.where(m_c, s_c, NEG), axis=-1) * m_c
    o_c = jnp.einsum('btgrc,bcgd->btgrd', p_c.astype(q.dtype), kvc[:, :, 1])
    imp = jnp.pad(p_c.sum(axis=3), ((0, 0), (0, 0), (0, 0), (0, ns * BLK_RATIO - nc)))
    imp = imp.reshape(imp.shape[:3] + (ns, BLK_RATIO)).sum(-1)
    blk = jnp.arange(ns)
    cur = qpos // SLC_BLOCK
    forced = (blk[None, :] == cur[:, None]) | (blk[None, :] == 0)
    future = blk[None, :] > cur[:, None]
    imp = jnp.where(forced[None, :, None, :], FORCE, imp)
    imp = jnp.where(future[None, :, None, :], NEG, imp)
    top_val, idx = lax.top_k(imp, min(TOP_N, ns))
    k_sel, v_sel = gather_sel(idx)
    tok = idx[..., None] * SLC_BLOCK + jnp.arange(SLC_BLOCK)
    m_s = ((top_val > 0.5 * NEG)[..., None] & (tok <= qpos[None, :, None, None, None]))[:, :, :, None]
    s_s = jnp.einsum('btgrd,btgnld->btgrnl', qs, k_sel).astype(jnp.float32)
    s_s = jnp.where(m_s, s_s, NEG)
    shp = s_s.shape
    p_s = jax.nn.softmax(s_s.reshape(shp[:4] + (-1,)), axis=-1).reshape(shp)
    o_s = jnp.einsum('btgrnl,btgnld->btgrd', p_s.astype(q.dtype), v_sel)
    s_w = jnp.einsum('btgrd,blgd->btgrl', qs, kvw[:, :, 0]).astype(jnp.float32)
    m_w = ((wpos[None, :] <= qpos[:, None]) & (wpos[None, :] > qpos[:, None] - WINDOW) & (wpos[None, :] >= 0))[None, :, None, None, :]
    p_w = jax.nn.softmax(jnp.where(m_w, s_w, NEG), axis=-1)
    o_w = jnp.einsum('btgrl,blgd->btgrd', p_w.astype(q.dtype), kvw[:, :, 1])
    return gates[..., 0:1] * o_c + gates[..., 1:2] * o_s + gates[..., 2:3] * o_w


def nsa_prompt(q, kv, gates, pe, w):
    b, t = q.shape[:2]
    kvc = compress_blocks(kv[:, :, 0], pe, w)
    slc = kv[:, :, 1]
    win = jnp.pad(kv[:, :, 2], ((0, 0), (WINDOW, 0), (0, 0), (0, 0), (0, 0)))
    ns = -(-t // SLC_BLOCK)
    bidx = jnp.arange(b)[:, None, None, None, None]
    gidx = jnp.arange(NSA_KV_HEADS)[None, None, :, None, None]

    def gather_sel(idx):
        rows = jnp.minimum(idx[..., None] * SLC_BLOCK + jnp.arange(SLC_BLOCK), t - 1)
        return slc[bidx, rows, 0, gidx], slc[bidx, rows, 1, gidx]

    nqb = t // Q_BLOCK
    qb = q.reshape((b, nqb, Q_BLOCK) + q.shape[2:]).swapaxes(0, 1)
    gb = gates.reshape((b, nqb, Q_BLOCK) + gates.shape[2:]).swapaxes(0, 1)

    def one_block(args):
        i, q_i, g_i = args
        start = i * Q_BLOCK
        qpos = start + jnp.arange(Q_BLOCK)
        kvw = lax.dynamic_slice_in_dim(win, start, WINDOW + Q_BLOCK, axis=1)
        wpos = start - WINDOW + jnp.arange(WINDOW + Q_BLOCK)
        return nsa_attend(q_i, qpos, kvc, gather_sel, ns, kvw, wpos, g_i)

    out = lax.map(one_block, (jnp.arange(nqb), qb, gb))
    return out.swapaxes(0, 1).reshape(b, t, NSA_W)


def nsa_sample(q, kv, gates, pe, w, pool_cmp, pool_slc, win_buf, page_table):
    b, s = q.shape[:2]
    n_pages = page_table.shape[1]
    past = n_pages * PAGE_SIZE
    cmp_past = pool_cmp[page_table].reshape(b, past, 2, NSA_KV_HEADS, HEAD_DIM)
    kvc = compress_blocks(cmp_past, pe, w)
    n_new_full = s // CMP_BLOCK
    if n_new_full > 0:
        kvc = jnp.concatenate([kvc, compress_blocks(kv[:, :n_new_full * CMP_BLOCK, 0], pe, w)], axis=1)
    ns = -(-(past + s) // SLC_BLOCK)
    new_slc = kv[:, :, 1]
    bidx = jnp.arange(b)[:, None, None, None, None]
    gidx = jnp.arange(NSA_KV_HEADS)[None, None, :, None, None]

    def gather_sel(idx):
        rows = idx[..., None] * SLC_BLOCK + jnp.arange(SLC_BLOCK)
        phys = page_table[bidx, jnp.minimum(rows // PAGE_SIZE, n_pages - 1)]
        off = rows % PAGE_SIZE
        rn = jnp.clip(rows - past, 0, s - 1)
        in_past = (rows < past)[..., None]
        k_sel = jnp.where(in_past, pool_slc[phys, off, 0, gidx], new_slc[bidx, rn, 0, gidx])
        v_sel = jnp.where(in_past, pool_slc[phys, off, 1, gidx], new_slc[bidx, rn, 1, gidx])
        return k_sel, v_sel

    wb = win_buf.shape[1]
    kvw = jnp.concatenate([win_buf, kv[:, :, 2]], axis=1)
    wpos = past - wb + jnp.arange(wb + s)
    qpos = past + jnp.arange(s)
    out = nsa_attend(q, qpos, kvc, gather_sel, ns, kvw, wpos, gates)
    return out.reshape(b, s, NSA_W)


def gmlp_mix(u, v, ws, bs):
    b, t = u.shape[:2]
    nck = -(-t // CHUNK)
    vp = jnp.pad(v, ((0, 0), (0, nck * CHUNK - t), (0, 0))).reshape(b, nck, CHUNK, GM_HEADS, HEAD_DIM)
    wm = ws * jnp.tril(jnp.ones((CHUNK, CHUNK), ws.dtype))
    mix = jnp.einsum('hpq,bcqhd->bcphd', wm, vp) + bs.T[None, None, :, :, None]
    return u * mix.reshape(b, nck * CHUNK, GM_W)[:, :t]


def rwkv_mix(zr, shift_prev, s0, mu, vec, w2, a2, g2, rk):
    f32 = jnp.float32
    b, t = zr.shape[:2]
    ext = jnp.concatenate([shift_prev[:, None].astype(zr.dtype), zr], axis=1)
    zs = zr + (ext[:, :-1] - zr) * mu
    o1, o2, o3 = RW_W, 2 * RW_W, 3 * RW_W
    o4 = o3 + W_LORA
    o5 = o4 + A_LORA
    zs = zs.astype(f32)
    r, k, v = zs[..., :o1], zs[..., o1:o2], zs[..., o2:o3]
    w0, a0, k_k, k_a, ln_g, ln_b = [vec[i].astype(f32) for i in range(6)]
    wlog = -jax.nn.softplus(-(w0 + jnp.tanh(zs[..., o3:o4]) @ w2.astype(f32))) - 0.5
    decay = jnp.exp(-jnp.exp(wlog))
    a = jax.nn.sigmoid(a0 + zs[..., o4:o5] @ a2.astype(f32))
    g = jax.nn.sigmoid(zs[..., o5:]) @ g2.astype(f32)
    hd = lambda arr: arr.reshape(b, t, RW_HEADS, HEAD_DIM)
    kk = hd(k * k_k)
    kk = kk / jnp.maximum(jnp.sqrt(jnp.sum(kk * kk, axis=-1, keepdims=True)), 1e-12)
    k = hd(k * (1.0 + (a - 1.0) * k_a))
    r, v, decay, a = hd(r), hd(v), hd(decay), hd(a)

    def step(S, inp):
        r_t, w_t, k_t, v_t, a_t, b_t = inp
        sa = jnp.einsum('bhvk,bhk->bhv', S, a_t)
        S = S * w_t[:, :, None, :] + sa[..., None] * b_t[:, :, None, :] + v_t[..., None] * k_t[:, :, None, :]
        return S, jnp.einsum('bhvk,bhk->bhv', S, r_t)

    tm = lambda arr: jnp.swapaxes(arr, 0, 1)
    s_fin, ys = lax.scan(step, s0.astype(f32), (tm(r), tm(decay), tm(k), tm(v), tm(-kk), tm(kk * a)))
    y = tm(ys)
    ym = jnp.mean(y, axis=-1, keepdims=True)
    yv = jnp.mean(jnp.square(y - ym), axis=-1, keepdims=True)
    y = ((y - ym) * lax.rsqrt(yv + RW_LN_EPS)).reshape(b, t, RW_W) * ln_g + ln_b
    bonus = jnp.sum(r * k * rk.astype(f32), axis=-1, keepdims=True) * v
    y = (y + bonus.reshape(b, t, RW_W)) * g
    return y.astype(zr.dtype), ext[:, -1], s_fin.astype(s0.dtype)


def mem_kv_proj(mem, g, w_kv):
    b = mem.shape[0]
    return (rms_norm(mem, g) @ w_kv).reshape(b, N_MEM, 2, MEM_HEADS, MEM_HEAD_DIM)


def mem_attend(h, w_q, kv, w_o):
    b, t = h.shape[:2]
    q = (h @ w_q).reshape(b, t, MEM_HEADS, MEM_HEAD_DIM) * jnp.asarray(MEM_HEAD_DIM ** -0.5, h.dtype)
    s = jnp.einsum('bthd,bmhd->bhtm', q, kv[:, :, 0]).astype(jnp.float32)
    p = jax.nn.softmax(s, axis=-1).astype(h.dtype)
    o = jnp.einsum('bhtm,bmhd->bthd', p, kv[:, :, 1]).reshape(b, t, MEM_W)
    return o @ w_o


def conv_ffn(h, conv_prev, w_in, cw, cb, w_out):
    t = h.shape[1]
    gu = h @ w_in
    gate, up = gu[..., :D_FF], gu[..., D_FF:]
    ext = jnp.concatenate([conv_prev.astype(gate.dtype), gate], axis=1)
    conv = cb + sum(ext[:, i:i + t] * cw[i] for i in range(CONV_W))
    return (jax.nn.silu(conv) * up) @ w_out, ext[:, -(CONV_W - 1):]


def trunk_layer(x, pos, nsa_fn, shift_prev, rw_state, conv_prev, kv_mem, P):
    b, t = x.shape[:2]
    ng = P['norm_g']
    h = rms_norm(x, ng[0])
    z = h @ P['w_in']
    zg = jax.nn.gelu(z[..., OFF_GM:OFF_Q])
    u = zg[..., :GM_W]
    v = layer_norm(zg[..., GM_W:], P['gm_ln'][0], P['gm_ln'][1])
    o_gm = gmlp_mix(u, v, P['gm_ws'], P['gm_bs'])
    q = rope(z[..., OFF_Q:OFF_KV].reshape(b, t, NSA_HEADS, HEAD_DIM), pos)
    q = q.reshape(b, t, NSA_KV_HEADS, NSA_REP, HEAD_DIM)
    kv = z[..., OFF_KV:OFF_GATE].reshape(b, t, 3, 2, NSA_KV_HEADS, HEAD_DIM)
    kv = jnp.stack([rope(kv[:, :, :, 0], pos), kv[:, :, :, 1]], axis=3)
    gates = jax.nn.sigmoid(z[..., OFF_GATE:OFF_RW]).reshape(b, t, NSA_KV_HEADS, NSA_REP, 3)
    o_nsa = nsa_fn(q, kv, gates)
    o_rw, shift_new, rw_new = rwkv_mix(z[..., OFF_RW:], shift_prev, rw_state, P['rw_mu'], P['rw_vec'],
                                       P['rw_w2'], P['rw_a2'], P['rw_g2'], P['rw_rk'])
    mix = jnp.concatenate([o_gm, o_nsa.astype(x.dtype), o_rw], axis=-1) @ P['w_out']
    x = x + rms_norm(mix, ng[1])
    h = rms_norm(x, ng[2])
    x = x + rms_norm(mem_attend(h, P['w_mem_q'], kv_mem, P['w_mem_o']), ng[3])
    h = rms_norm(x, ng[4])
    f, conv_new = conv_ffn(h, conv_prev, P['ffn_w_in'], P['ffn_conv_w'], P['ffn_conv_b'], P['ffn_w_out'])
    x = x + rms_norm(f, ng[5])
    return x, (v, kv, shift_new, rw_new, conv_new)


def setup_inputs(seed: int = 0) -> dict:
    key = jax.random.key(seed)
    ks = jax.random.split(key, 40)
    f32 = jnp.float32
    nrm = lambda k, shape, sc: jax.random.normal(k, shape, f32) * sc
    n_pages = PAST_LEN // PAGE_SIZE
    n_used = DEC_BATCH * n_pages
    n_phys = n_used + n_used // 4
    wb = min(WINDOW, PAST_LEN)
    page_table = jax.random.permutation(ks[0], n_phys)[:n_used].reshape(DEC_BATCH, n_pages).astype(jnp.int32)
    rw_base = jnp.array([0.0, 0.0, 0.85, 1.0, 1.0, 0.0], f32)[None, :, None]
    rw_sc = jnp.array([0.5, 0.5, 0.05, 0.05, 0.05, 0.05], f32)[None, :, None]
    return {
        "x_prompt": nrm(ks[1], (BATCH, SEQ, D_MODEL), 1.0),
        "x_sample": nrm(ks[2], (DEC_BATCH, DEC_SEQ, D_MODEL), 1.0),
        "cache_cmp_kv": nrm(ks[3], (DEPTH, n_phys, PAGE_SIZE, 2, NSA_KV_HEADS, HEAD_DIM), 1.0),
        "cache_slc_kv": nrm(ks[4], (DEPTH, n_phys, PAGE_SIZE, 2, NSA_KV_HEADS, HEAD_DIM), 1.0),
        "cache_win_kv": nrm(ks[5], (DEPTH, DEC_BATCH, wb, 2, NSA_KV_HEADS, HEAD_DIM), 1.0),
        "cache_mem_kv": nrm(ks[6], (DEPTH, DEC_BATCH, N_MEM, 2, MEM_HEADS, MEM_HEAD_DIM), 1.0),
        "state_rwkv": nrm(ks[7], (DEPTH, DEC_BATCH, RW_HEADS, HEAD_DIM, HEAD_DIM), 0.1),
        "state_rwkv_shift": nrm(ks[8], (DEPTH, DEC_BATCH, RW_COLS), 1.0),
        "state_ffn_conv": nrm(ks[9], (DEPTH, DEC_BATCH, CONV_W - 1, D_FF), 1.0),
        "page_table": page_table,
        "mem_prompt": nrm(ks[10], (BATCH, N_MEM, D_MODEL), 1.0),
        "norm_g": 1.0 + nrm(ks[11], (DEPTH, 6, D_MODEL), 0.05),
        "w_in": nrm(ks[12], (DEPTH, D_MODEL, IN_COLS), D_MODEL ** -0.5),
        "w_out": nrm(ks[13], (DEPTH, MIX_W, D_MODEL), MIX_W ** -0.5),
        "gm_ln": jnp.array([1.0, 0.0], f32)[None, :, None] + nrm(ks[14], (DEPTH, 2, GM_W), 0.05),
        "gm_ws": nrm(ks[15], (DEPTH, GM_HEADS, CHUNK, CHUNK), CHUNK ** -0.5),
        "gm_bs": 1.0 + nrm(ks[16], (DEPTH, GM_HEADS, CHUNK), 0.05),
        "cmp_pe": nrm(ks[17], (DEPTH, CMP_BLOCK, 2, HEAD_DIM), 0.1),
        "cmp_w": nrm(ks[18], (DEPTH, CMP_BLOCK, 2, HEAD_DIM, HEAD_DIM), (CMP_BLOCK * HEAD_DIM) ** -0.5),
        "rw_mu": jax.random.uniform(ks[19], (DEPTH, RW_COLS), f32),
        "rw_vec": rw_base + rw_sc * jax.random.normal(ks[20], (DEPTH, 6, RW_W), f32),
        "rw_w2": nrm(ks[21], (DEPTH, W_LORA, RW_W), W_LORA ** -0.5),
        "rw_a2": nrm(ks[22], (DEPTH, A_LORA, RW_W), A_LORA ** -0.5),
        "rw_g2": nrm(ks[23], (DEPTH, G_LORA, RW_W), G_LORA ** -0.5),
        "rw_rk": nrm(ks[24], (DEPTH, RW_HEADS, HEAD_DIM), 0.1),
        "mem_g": 1.0 + nrm(ks[25], (DEPTH, D_MODEL), 0.05),
        "w_mem_q": nrm(ks[26], (DEPTH, D_MODEL, MEM_W), D_MODEL ** -0.5),
        "w_mem_kv": nrm(ks[27], (DEPTH, D_MODEL, 2 * MEM_W), D_MODEL ** -0.5),
        "w_mem_o": nrm(ks[28], (DEPTH, MEM_W, D_MODEL), MEM_W ** -0.5),
        "ffn_w_in": nrm(ks[29], (DEPTH, D_MODEL, 2 * D_FF), D_MODEL ** -0.5),
        "ffn_conv_w": nrm(ks[30], (DEPTH, CONV_W, D_FF), CONV_W ** -0.5),
        "ffn_conv_b": nrm(ks[31], (DEPTH, D_FF), 0.02),
        "ffn_w_out": nrm(ks[32], (DEPTH, D_FF, D_MODEL), D_FF ** -0.5),
    }


def reference(x_prompt, x_sample, cache_cmp_kv, cache_slc_kv, cache_win_kv, cache_mem_kv, state_rwkv,
              state_rwkv_shift, state_ffn_conv, page_table, mem_prompt, norm_g, w_in, w_out, gm_ln, gm_ws, gm_bs,
              cmp_pe, cmp_w, rw_mu, rw_vec, rw_w2, rw_a2, rw_g2, rw_rk, mem_g, w_mem_q, w_mem_kv, w_mem_o,
              ffn_w_in, ffn_conv_w, ffn_conv_b, ffn_w_out):
    bp, tp = x_prompt.shape[:2]
    ts = x_sample.shape[1]
    past = page_table.shape[1] * PAGE_SIZE
    pos_p = jnp.arange(tp)
    pos_s = past + jnp.arange(ts)
    wbp = min(WINDOW, tp)
    dt = x_prompt.dtype
    shift0 = jnp.zeros((bp, RW_COLS), dt)
    rw0 = jnp.zeros((bp, RW_HEADS, HEAD_DIM, HEAD_DIM), state_rwkv.dtype)
    conv0 = jnp.zeros((bp, CONV_W - 1, D_FF), dt)
    xp, xs = x_prompt, x_sample
    pc, psl, pw, prw, psh, pcv, pm = [], [], [], [], [], [], []
    sc, ssl, sw, srw, ssh, scv, sgv = [], [], [], [], [], [], []
    for l in range(DEPTH):
        P = {"norm_g": norm_g[l], "w_in": w_in[l], "w_out": w_out[l], "gm_ln": gm_ln[l], "gm_ws": gm_ws[l],
             "gm_bs": gm_bs[l], "rw_mu": rw_mu[l], "rw_vec": rw_vec[l], "rw_w2": rw_w2[l], "rw_a2": rw_a2[l],
             "rw_g2": rw_g2[l], "rw_rk": rw_rk[l], "w_mem_q": w_mem_q[l], "w_mem_o": w_mem_o[l],
             "ffn_w_in": ffn_w_in[l], "ffn_conv_w": ffn_conv_w[l], "ffn_conv_b": ffn_conv_b[l],
             "ffn_w_out": ffn_w_out[l]}
        kvm_p = mem_kv_proj(mem_prompt, mem_g[l], w_mem_kv[l])
        nsa_p = functools.partial(nsa_prompt, pe=cmp_pe[l], w=cmp_w[l])
        xp, (_, kv_p, sh_p, rw_p, cv_p) = trunk_layer(xp, pos_p, nsa_p, shift0, rw0, conv0, kvm_p, P)
        pc.append(kv_p[:, :, 0]); psl.append(kv_p[:, :, 1]); pw.append(kv_p[:, tp - wbp:, 2])
        prw.append(rw_p); psh.append(sh_p); pcv.append(cv_p); pm.append(kvm_p)
        nsa_s = functools.partial(nsa_sample, pe=cmp_pe[l], w=cmp_w[l], pool_cmp=cache_cmp_kv[l],
                                  pool_slc=cache_slc_kv[l], win_buf=cache_win_kv[l], page_table=page_table)
        xs, (v_s, kv_s, sh_s, rw_s, cv_s) = trunk_layer(xs, pos_s, nsa_s, state_rwkv_shift[l], state_rwkv[l],
                                                         state_ffn_conv[l], cache_mem_kv[l], P)
        sc.append(kv_s[:, :, 0]); ssl.append(kv_s[:, :, 1]); sw.append(kv_s[:, :, 2])
        srw.append(rw_s); ssh.append(sh_s); scv.append(cv_s); sgv.append(v_s)
    p_cmp_kv, p_slc_kv, p_win_kv = jnp.stack(pc), jnp.stack(psl), jnp.stack(pw)
    p_rwkv, p_rwkv_shift, p_ffn_conv, p_mem_kv = jnp.stack(prw), jnp.stack(psh), jnp.stack(pcv), jnp.stack(pm)
    s_cmp_kv, s_slc_kv, s_win_kv = jnp.stack(sc), jnp.stack(ssl), jnp.stack(sw)
    s_rwkv, s_rwkv_shift, s_ffn_conv, s_gmlp_v = jnp.stack(srw), jnp.stack(ssh), jnp.stack(scv), jnp.stack(sgv)
    return (xp, xs, p_cmp_kv, p_slc_kv, p_win_kv, p_rwkv, p_rwkv_shift, p_ffn_conv, p_mem_kv,
            s_cmp_kv, s_slc_kv, s_win_kv, s_rwkv, s_rwkv_shift, s_ffn_conv, s_gmlp_v)
```

```python
import functools
import math

import jax
import jax.numpy as jnp
from jax import lax
from jax.experimental import pallas as pl
from jax.experimental.pallas import tpu as pltpu

F32 = jnp.float32
BF16 = jnp.bfloat16

D_MODEL = 1024
HEAD_DIM = 64
GM_W = 256
GM_HEADS = 4
CHUNK = 128
NSA_W = 512
NSA_HEADS = 8
NSA_KV_HEADS = 2
NSA_REP = 4
CMP_BLOCK = 32
SLC_BLOCK = 64
TOP_N = 16
WINDOW = 512
RW_W = 256
RW_HEADS = 4
W_LORA = 64
A_LORA = 64
G_LORA = 128
RW_COLS = 3 * RW_W + W_LORA + A_LORA + G_LORA
RW_LN_EPS = 64e-5
N_MEM = 256
MEM_HEADS = 4
MEM_HEAD_DIM = 256
D_FF = 2816
CONV_W = 3
PAGE_SIZE = 128
ROPE_THETA = 10000.0
NORM_EPS = 1e-6
LN_EPS = 1e-5
NEG = -1e30
FORCE = 1e4
OFF_GM = 0
OFF_Q = 512
OFF_KV = 1024
OFF_GATE = 1792
OFF_RW = 1816
IN_COLS = 2840
KV_COLS = 2 * NSA_KV_HEADS * HEAD_DIM
CMP_K = CMP_BLOCK * KV_COLS
GATE_PAD = 128
PACK_COLS = OFF_GATE + RW_COLS + GATE_PAD

LANES = 128
VMEM_LIMIT = 56 * 1024 * 1024


def _cparams(sem):
    return pltpu.CompilerParams(dimension_semantics=sem, vmem_limit_bytes=VMEM_LIMIT)


def _rms(x, g):
    return x * lax.rsqrt(jnp.mean(x * x, axis=-1, keepdims=True) + NORM_EPS) * g


def _dot(a, b):
    return jnp.dot(a.astype(BF16), b.astype(BF16), preferred_element_type=F32)


def _dot_nt(a, b):
    return lax.dot_general(a.astype(BF16), b.astype(BF16), (((1,), (1,)), ((), ())),
                           preferred_element_type=F32)


def _lane_lo(shape):
    return lax.broadcasted_iota(jnp.int32, shape, len(shape) - 1) % LANES < HEAD_DIM


def _head_sum(x):
    cols = []
    for c in range(x.shape[1] // LANES):
        xc = x[:, c * LANES:(c + 1) * LANES]
        lo_mask = _lane_lo(xc.shape)
        lo = jnp.sum(jnp.where(lo_mask, xc, 0.0), axis=-1, keepdims=True)
        hi = jnp.sum(jnp.where(lo_mask, 0.0, xc), axis=-1, keepdims=True)
        cols.append(jnp.where(lo_mask, lo, hi))
    return jnp.concatenate(cols, axis=1)


def _inproj_kernel(x_ref, g_ref, w_ref, cos_ref, sin_ref, ln_ref, ws_ref, bias_ref,
                   ogm_ref, vgm_ref, qpad_ref, kvc_ref, kvs_ref, kvw_ref, kvb_ref, gate_ref, zr_ref,
                   *, single):
    tm = x_ref.shape[0]
    h = _rms(x_ref[...], g_ref[...])
    z = _dot(h, w_ref[...])
    zg = jax.nn.gelu(z[:, 0:2 * GM_W])
    u = zg[:, :GM_W]
    vv = zg[:, GM_W:]
    mean = jnp.mean(vv, axis=-1, keepdims=True)
    var = jnp.mean(jnp.square(vv - mean), axis=-1, keepdims=True)
    v = (vv - mean) * lax.rsqrt(var + LN_EPS) * ln_ref[0:1, :] + ln_ref[1:2, :]
    vgm_ref[...] = v
    if single:
        ogm_ref[...] = (u * (v * ws_ref[...] + bias_ref[...])).astype(BF16)
    else:
        head_of_lane = lax.broadcasted_iota(jnp.int32, (CHUNK, GM_W), 1) // HEAD_DIM
        causal = (lax.broadcasted_iota(jnp.int32, (CHUNK, CHUNK), 0)
                  >= lax.broadcasted_iota(jnp.int32, (CHUNK, CHUNK), 1))
        wms = [jnp.where(causal, ws_ref[hh], 0.0).astype(BF16) for hh in range(GM_HEADS)]
        for c in range(tm // CHUNK):
            vc = v[c * CHUNK:(c + 1) * CHUNK]
            acc = bias_ref[...]
            for hh in range(GM_HEADS):
                vm = jnp.where(head_of_lane == hh, vc, 0.0).astype(BF16)
                acc = acc + jnp.dot(wms[hh], vm, preferred_element_type=F32)
            ogm_ref[c * CHUNK:(c + 1) * CHUNK, :] = (u[c * CHUNK:(c + 1) * CHUNK] * acc).astype(BF16)
    cos = cos_ref[...]
    sin = sin_ref[...]
    first_half = lax.broadcasted_iota(jnp.int32, (tm, LANES), 1) % HEAD_DIM < HEAD_DIM // 2
    lane_lo = _lane_lo((tm, LANES))

    def rope(xc):
        rot = jnp.where(first_half, pltpu.roll(xc, LANES - HEAD_DIM // 2, 1), pltpu.roll(xc, HEAD_DIM // 2, 1))
        return xc * cos + rot * sin

    scale = HEAD_DIM ** -0.5
    for c in range(NSA_W // LANES):
        qc = rope(z[:, OFF_Q + c * LANES:OFF_Q + (c + 1) * LANES]) * scale
        qr = pltpu.roll(qc, HEAD_DIM, 1)
        if (2 * c) // NSA_REP == 0:
            even = jnp.where(lane_lo, qc, 0.0)
            odd = jnp.where(lane_lo, qr, 0.0)
        else:
            even = jnp.where(lane_lo, 0.0, qr)
            odd = jnp.where(lane_lo, 0.0, qc)
        qpad_ref[:, (2 * c) * LANES:(2 * c + 1) * LANES] = even.astype(BF16)
        qpad_ref[:, (2 * c + 1) * LANES:(2 * c + 2) * LANES] = odd.astype(BF16)
    for j, ref in enumerate((kvc_ref, kvs_ref, kvw_ref)):
        base = OFF_KV + j * KV_COLS
        kk = rope(z[:, base:base + LANES])
        vj = z[:, base + LANES:base + KV_COLS]
        ref[:, 0:LANES] = kk
        ref[:, LANES:KV_COLS] = vj
        if j >= 1:
            kvb_ref[:, (j - 1) * KV_COLS:(j - 1) * KV_COLS + LANES] = kk.astype(BF16)
            kvb_ref[:, (j - 1) * KV_COLS + LANES:j * KV_COLS] = vj.astype(BF16)
    zr_ref[...] = z[:, OFF_GATE:OFF_GATE + RW_COLS]
    gate_ref[...] = jax.nn.sigmoid(z[:, OFF_GATE + RW_COLS:])


def _pack_w_in(w_in):
    gate = w_in[:, OFF_GATE:OFF_RW].reshape(D_MODEL, NSA_HEADS, 3)
    gate = jnp.transpose(gate, (0, 2, 1)).reshape(D_MODEL, 3 * NSA_HEADS)
    gate = jnp.pad(gate, ((0, 0), (0, GATE_PAD - 3 * NSA_HEADS)))
    return jnp.concatenate([w_in[:, :OFF_GATE], w_in[:, OFF_RW:], gate], axis=1).astype(BF16)


def _in_proj(x, g0, w_pack, cos, sin, gm_ln, gm_ws, gm_bs, *, seq_len, tm):
    n = x.shape[0]
    single = seq_len == 1
    if single:
        ws_arg = jnp.repeat(gm_ws[:, 0, 0], HEAD_DIM)[None, :]
        bias_arg = jnp.repeat(gm_bs[:, 0], HEAD_DIM)[None, :]
        ws_spec = pl.BlockSpec((1, GM_W), lambda i: (0, 0))
        bias_spec = pl.BlockSpec((1, GM_W), lambda i: (0, 0))
        tab_map = lambda i: (i, 0)
    else:
        assert seq_len % tm == 0 and tm % CHUNK == 0
        ws_arg = gm_ws
        bias_arg = jnp.repeat(gm_bs.T, HEAD_DIM, axis=1)
        ws_spec = pl.BlockSpec((GM_HEADS, CHUNK, CHUNK), lambda i: (0, 0, 0))
        bias_spec = pl.BlockSpec((CHUNK, GM_W), lambda i: (0, 0))
        tiles = seq_len // tm
        tab_map = lambda i: (i % tiles, 0)
    row = lambda w: pl.BlockSpec((tm, w), lambda i: (i, 0))
    outs = [(GM_W, BF16), (GM_W, F32), (NSA_HEADS * LANES, BF16), (KV_COLS, F32), (KV_COLS, F32),
            (KV_COLS, F32), (2 * KV_COLS, BF16), (GATE_PAD, F32), (RW_COLS, F32)]
    return pl.pallas_call(
        functools.partial(_inproj_kernel, single=single),
        grid=(n // tm,),
        in_specs=[row(D_MODEL), pl.BlockSpec((1, D_MODEL), lambda i: (0, 0)),
                  pl.BlockSpec((D_MODEL, PACK_COLS), lambda i: (0, 0)),
                  pl.BlockSpec((tm, LANES), tab_map), pl.BlockSpec((tm, LANES), tab_map),
                  pl.BlockSpec((2, GM_W), lambda i: (0, 0)), ws_spec, bias_spec],
        out_specs=[row(w) for w, _ in outs],
        out_shape=[jax.ShapeDtypeStruct((n, w), dt) for w, dt in outs],
        compiler_params=_cparams(("parallel",)),
        name="in_proj",
    )(x, g0[None, :], w_pack, cos, sin, gm_ln, ws_arg, bias_arg)


def _rope_tables(pos):
    half = HEAD_DIM // 2
    inv = ROPE_THETA ** (-jnp.arange(half, dtype=F32) / half)
    ang = pos.astype(F32)[:, None] * inv[None, :]
    cos, sin = jnp.cos(ang), jnp.sin(ang)
    return jnp.tile(cos, (1, 4)), jnp.tile(jnp.concatenate([-sin, sin], axis=1), (1, 2))


def _compress_kernel(x_ref, pe_ref, w_ref, o_ref):
    o_ref[...] = _dot(x_ref[...] + pe_ref[...], w_ref[...]).astype(o_ref.dtype)


def _compress_weights(cmp_pe, cmp_w):
    eye = jnp.eye(2 * NSA_KV_HEADS, dtype=F32).reshape(2, NSA_KV_HEADS, 2, NSA_KV_HEADS)
    wb = jnp.einsum('lsde,sgtq->lsgdtqe', cmp_w, eye).reshape(CMP_K, KV_COLS).astype(BF16)
    pe = jnp.broadcast_to(cmp_pe[:, :, None, :], (CMP_BLOCK, 2, NSA_KV_HEADS, HEAD_DIM)).reshape(1, CMP_K)
    return pe, wb


def _compress(x, pe, wb, *, tr, out_dtype):
    n = x.shape[0]
    return pl.pallas_call(
        _compress_kernel,
        grid=(n // tr,),
        in_specs=[pl.BlockSpec((tr, CMP_K), lambda i: (i, 0)), pl.BlockSpec((1, CMP_K), lambda i: (0, 0)),
                  pl.BlockSpec((CMP_K, KV_COLS), lambda i: (0, 0))],
        out_specs=pl.BlockSpec((tr, KV_COLS), lambda i: (i, 0)),
        out_shape=jax.ShapeDtypeStruct((n, KV_COLS), out_dtype),
        compiler_params=_cparams(("parallel",)),
        name="nsa_compress",
    )(x, pe, wb)


def _group_q(qpad_ref, g):
    return jnp.concatenate([qpad_ref[:, (g * NSA_REP + r) * LANES:(g * NSA_REP + r + 1) * LANES]
                            for r in range(NSA_REP)], axis=0)


def _assemble_heads(pieces, tq):
    lane_lo = _lane_lo((tq, LANES))
    cols = []
    for c in range(NSA_W // LANES):
        g = (2 * c) // NSA_REP
        r = (2 * c) % NSA_REP
        even = pieces[g][r * tq:(r + 1) * tq]
        odd = pieces[g][(r + 1) * tq:(r + 2) * tq]
        if g == 0:
            cols.append(jnp.where(lane_lo, even, pltpu.roll(odd, HEAD_DIM, 1)))
        else:
            cols.append(jnp.where(lane_lo, pltpu.roll(even, HEAD_DIM, 1), odd))
    return cols


def _cmp_attn_kernel(qpad_ref, kvc_ref, oc_ref, sel_ref):
    tq = qpad_ref.shape[0]
    q0 = pl.program_id(1) * tq
    kvc = kvc_ref[0]
    nc = kvc.shape[0]
    ns = nc // 2
    coli = lax.broadcasted_iota(jnp.int32, (1, nc), 1)
    cblk = jnp.where(coli < ns, 2 * coli, 2 * (coli - ns) + 1)
    c_end = cblk * CMP_BLOCK + (CMP_BLOCK - 1)
    rowpos = q0 + lax.broadcasted_iota(jnp.int32, (NSA_REP * tq, 1), 0) % tq
    m_c = c_end <= rowpos
    blk = lax.broadcasted_iota(jnp.int32, (ns, tq), 0)
    cur = (q0 + lax.broadcasted_iota(jnp.int32, (ns, tq), 1)) // SLC_BLOCK
    pieces = []
    for g in range(NSA_KV_HEADS):
        qg = _group_q(qpad_ref, g)
        s = _dot_nt(qg, kvc[:, 0:LANES])
        sm = jnp.where(m_c, s, NEG)
        e = jnp.exp(sm - jnp.max(sm, axis=-1, keepdims=True))
        p = jnp.where(m_c, e / jnp.sum(e, axis=-1, keepdims=True), 0.0)
        pieces.append(_dot(p, kvc[:, LANES:KV_COLS]))
        ps = p[0:tq] + p[tq:2 * tq] + p[2 * tq:3 * tq] + p[3 * tq:4 * tq]
        imp = (ps[:, :ns] + ps[:, ns:]).T
        imp = jnp.where((blk == cur) | (blk == 0), FORCE, imp)
        imp = jnp.where(blk > cur, NEG, imp)
        work = imp
        sel = jnp.zeros((ns, tq), F32)
        for _ in range(min(TOP_N, ns)):
            mx = jnp.max(work, axis=0, keepdims=True)
            first = jnp.min(jnp.where(work == mx, blk, ns), axis=0, keepdims=True)
            pick = blk == first
            sel = jnp.where(pick, 1.0, sel)
            work = jnp.where(pick, -jnp.inf, work)
        sel = jnp.where(imp > 0.5 * NEG, sel, 0.0)
        sel_ref[:, g * ns:(g + 1) * ns] = sel.T.astype(BF16)
    cols = _assemble_heads(pieces, tq)
    for c in range(NSA_W // LANES):
        oc_ref[:, c * LANES:(c + 1) * LANES] = cols[c]


def _cmp_attn(qpad, kvc_perm, *, batch, seq_len, tq):
    n = qpad.shape[0]
    nq = seq_len // tq
    nc = kvc_perm.shape[1]
    return pl.pallas_call(
        _cmp_attn_kernel,
        grid=(batch, nq),
        in_specs=[pl.BlockSpec((tq, NSA_HEADS * LANES), lambda b, i: (b * nq + i, 0)),
                  pl.BlockSpec((1, nc, KV_COLS), lambda b, i: (b, 0, 0))],
        out_specs=[pl.BlockSpec((tq, NSA_W), lambda b, i: (b * nq + i, 0)),
                   pl.BlockSpec((tq, nc), lambda b, i: (b * nq + i, 0))],
        out_shape=[jax.ShapeDtypeStruct((n, NSA_W), F32), jax.ShapeDtypeStruct((n, nc), BF16)],
        compiler_params=_cparams(("parallel", "parallel")),
        name="nsa_cmp_attn",
    )(qpad, kvc_perm)


SEL_KV_TILE = 512


def _gate_col(gate, br, c, tq):
    lane_lo = _lane_lo((tq, LANES))
    i0 = br * NSA_HEADS + 2 * c
    return jnp.where(lane_lo, gate[:, i0:i0 + 1], gate[:, i0 + 1:i0 + 2])


def _sel_win_kernel(qpad_ref, sel_ref, gate_ref, oc_ref, kvb_ref, out_ref, *, seq_len):
    tq = qpad_ref.shape[0]
    rows = NSA_REP * tq
    kt = min(SEL_KV_TILE, seq_len)
    ns = seq_len // SLC_BLOCK
    q0 = pl.program_id(1) * tq
    nkv = (q0 + tq + kt - 1) // kt
    rowpos = q0 + lax.broadcasted_iota(jnp.int32, (rows, 1), 0) % tq
    blk_r = lax.broadcasted_iota(jnp.int32, (ns, kt), 0)
    blk_c = lax.broadcasted_iota(jnp.int32, (ns, kt), 1) // SLC_BLOCK
    lane_kt = lax.broadcasted_iota(jnp.int32, (1, kt), 1)
    n_win = (WINDOW + tq) // LANES
    sel_pieces, win_pieces = [], []
    for g in range(NSA_KV_HEADS):
        qg = _group_q(qpad_ref, g)
        selg = sel_ref[:, g * ns:(g + 1) * ns]
        sel4 = jnp.concatenate([selg] * NSA_REP, axis=0)

        def body(j, carry):
            m, l, acc = carry
            start = pl.multiple_of(j * kt, kt)
            k = kvb_ref[0, pl.ds(start, kt), 0:LANES]
            v = kvb_ref[0, pl.ds(start, kt), LANES:KV_COLS]
            s = _dot_nt(qg, k)
            expand = (blk_r == blk_c + j * (kt // SLC_BLOCK)).astype(BF16)
            chosen = jnp.dot(sel4, expand, preferred_element_type=F32)
            valid = (chosen > 0.5) & (start + lane_kt <= rowpos)
            s = jnp.where(valid, s, NEG)
            m_new = jnp.maximum(m, jnp.max(s, axis=-1, keepdims=True))
            alpha = jnp.exp(m - m_new)
            p = jnp.where(valid, jnp.exp(s - m_new), 0.0)
            l = alpha * l + jnp.sum(p, axis=-1, keepdims=True)
            acc = alpha * acc + _dot(p, v)
            return m_new, l, acc

        init = (jnp.full((rows, 1), NEG, F32), jnp.zeros((rows, 1), F32), jnp.zeros((rows, LANES), F32))
        _, l, acc = lax.fori_loop(0, nkv, body, init)
        sel_pieces.append(acc / l)
        ss, vs, valids = [], [], []
        for j in range(n_win):
            start_true = q0 - WINDOW + j * LANES
            start = pl.multiple_of(jnp.maximum(start_true, 0), LANES)
            k = kvb_ref[0, pl.ds(start, LANES), KV_COLS:KV_COLS + LANES]
            vs.append(kvb_ref[0, pl.ds(start, LANES), KV_COLS + LANES:2 * KV_COLS])
            kpos = start_true + lax.broadcasted_iota(jnp.int32, (1, LANES), 1)
            valids.append((kpos <= rowpos) & (kpos > rowpos - WINDOW) & (kpos >= 0))
            ss.append(_dot_nt(qg, k))
        s = jnp.concatenate(ss, axis=1)
        valid = jnp.concatenate(valids, axis=1)
        s = jnp.where(valid, s, NEG)
        e = jnp.where(valid, jnp.exp(s - jnp.max(s, axis=-1, keepdims=True)), 0.0)
        p = e / jnp.sum(e, axis=-1, keepdims=True)
        win_pieces.append(_dot(p, jnp.concatenate(vs, axis=0)))
    sel_cols = _assemble_heads(sel_pieces, tq)
    win_cols = _assemble_heads(win_pieces, tq)
    gate = gate_ref[...]
    for c in range(NSA_W // LANES):
        oc = oc_ref[:, c * LANES:(c + 1) * LANES]
        o = (_gate_col(gate, 0, c, tq) * oc + _gate_col(gate, 1, c, tq) * sel_cols[c]
             + _gate_col(gate, 2, c, tq) * win_cols[c])
        out_ref[:, c * LANES:(c + 1) * LANES] = o.astype(BF16)


def _sel_win(qpad, sel, gates, oc, kvb, *, batch, seq_len, tq):
    n = qpad.shape[0]
    nq = seq_len // tq
    ns2 = sel.shape[1]
    row = lambda w: pl.BlockSpec((tq, w), lambda b, i: (b * nq + i, 0))
    return pl.pallas_call(
        functools.partial(_sel_win_kernel, seq_len=seq_len),
        grid=(batch, nq),
        in_specs=[row(NSA_HEADS * LANES), row(ns2), row(GATE_PAD), row(NSA_W),
                  pl.BlockSpec((1, seq_len, 2 * KV_COLS), lambda b, i: (b, 0, 0))],
        out_specs=row(NSA_W),
        out_shape=jax.ShapeDtypeStruct((n, NSA_W), BF16),
        compiler_params=_cparams(("parallel", "parallel")),
        name="nsa_sel_win",
    )(qpad, sel, gates, oc, kvb.reshape(batch, seq_len, 2 * KV_COLS))


def _rwprep_kernel(zr_ref, prev_ref, halo_ref, mu_ref, vec_ref, w2_ref, a2_ref, g2_ref, rk_ref,
                   r_o, w_o, k_o, v_o, a_o, b_o, g_o, bonus_o, *, single, tiles_per_seq):
    tm = zr_ref.shape[0]
    zr = zr_ref[...]
    if single:
        prev = prev_ref[...]
    else:
        first_tile = pl.program_id(0) % tiles_per_seq == 0
        before = jnp.where(first_tile, prev_ref[0], halo_ref[7:8, :])
        rowi = lax.broadcasted_iota(jnp.int32, (tm, 1), 0)
        prev = jnp.where(rowi == 0, before, pltpu.roll(zr, 1, 0))
    zs = zr + (prev - zr) * mu_ref[...]
    r = zs[:, 0:RW_W]
    k = zs[:, RW_W:2 * RW_W]
    v = zs[:, 2 * RW_W:3 * RW_W]
    lora = zs[:, 3 * RW_W:3 * RW_W + W_LORA + A_LORA]
    w0, a0, k_k, k_a = (vec_ref[i:i + 1, :] for i in range(4))
    wlog = -jax.nn.softplus(-(w0 + _dot(jnp.tanh(lora), w2_ref[...]))) - 0.5
    decay = jnp.exp(-jnp.exp(wlog))
    a = jax.nn.sigmoid(a0 + _dot(lora, a2_ref[...]))
    g = _dot(jax.nn.sigmoid(zs[:, 3 * RW_W + W_LORA + A_LORA:]), g2_ref[...])
    kk = k * k_k
    kk = kk / jnp.maximum(jnp.sqrt(_head_sum(kk * kk)), 1e-12)
    k2 = k * (1.0 + (a - 1.0) * k_a)
    r_o[...] = r
    w_o[...] = decay
    k_o[...] = k2
    v_o[...] = v
    a_o[...] = -kk
    b_o[...] = kk * a
    g_o[...] = g
    bonus_o[...] = _head_sum(r * k2 * rk_ref[...]) * v


def _rw_prep(zr, shift_prev, P, *, seq_len, tm):
    n = zr.shape[0]
    single = seq_len == 1
    w2p = jnp.concatenate([P['rw_w2'], jnp.zeros((A_LORA, RW_W), F32)], axis=0).astype(BF16)
    a2p = jnp.concatenate([jnp.zeros((W_LORA, RW_W), F32), P['rw_a2']], axis=0).astype(BF16)
    const = lambda shape: pl.BlockSpec(shape, lambda i: (0,) * len(shape))
    if single:
        prev_arg = shift_prev
        prev_spec = pl.BlockSpec((tm, RW_COLS), lambda i: (i, 0))
        tiles = 1
    else:
        tiles = seq_len // tm
        prev_arg = shift_prev[:, None, :]
        prev_spec = pl.BlockSpec((1, 1, RW_COLS), lambda i: (i // tiles, 0, 0))
    halo_spec = pl.BlockSpec((8, RW_COLS), lambda i: (jnp.maximum(i * (tm // 8) - 1, 0), 0))
    row = pl.BlockSpec((tm, RW_W), lambda i: (i, 0))
    return pl.pallas_call(
        functools.partial(_rwprep_kernel, single=single, tiles_per_seq=tiles),
        grid=(n // tm,),
        in_specs=[pl.BlockSpec((tm, RW_COLS), lambda i: (i, 0)), prev_spec, halo_spec,
                  const((1, RW_COLS)), const((6, RW_W)), const((W_LORA + A_LORA, RW_W)),
                  const((W_LORA + A_LORA, RW_W)), const((G_LORA, RW_W)), const((1, RW_W))],
        out_specs=[row] * 8,
        out_shape=[jax.ShapeDtypeStruct((n, RW_W), F32)] * 8,
        compiler_params=_cparams(("parallel",)),
        name="rwkv_prep",
    )(zr, prev_arg, zr, P['rw_mu'][None, :], P['rw_vec'], w2p, a2p, P['rw_g2'].astype(BF16),
      P['rw_rk'].reshape(1, RW_W))


def _segsum(x, ones2):
    hi = x.astype(BF16)
    lo = (x - hi.astype(F32)).astype(BF16)
    return jnp.dot(jnp.concatenate([hi, lo], axis=1), ones2, preferred_element_type=F32)


def _scan_kernel(r_ref, w_ref, k_ref, v_ref, a_ref, b_ref, s0_ref, y_ref, sfin_ref, s_scr):
    nb, tc = r_ref.shape[0], r_ref.shape[1]
    npair = RW_HEADS // 2
    units = [(b, hp) for b in range(nb) for hp in range(npair)]

    @pl.when(pl.program_id(0) == 0)
    def _():
        s_scr[...] = s0_ref[...]

    ri = lax.broadcasted_iota(jnp.int32, (2 * LANES, LANES), 0) % LANES
    ci = lax.broadcasted_iota(jnp.int32, (2 * LANES, LANES), 1)
    ones2 = (ri // HEAD_DIM == ci // HEAD_DIM).astype(BF16)
    diag = (lax.broadcasted_iota(jnp.int32, (HEAD_DIM, LANES), 0)
            == lax.broadcasted_iota(jnp.int32, (HEAD_DIM, LANES), 1) % HEAD_DIM)

    nu = len(units)
    refs = dict(r=r_ref, w=w_ref, k=k_ref, v=v_ref, a=a_ref, b=b_ref)

    def run(base, nsteps):
        blocks = {name: [ref[b, pl.ds(base, nsteps), hp * LANES:(hp + 1) * LANES] for b, hp in units]
                  for name, ref in refs.items()}
        states = [s_scr[b, hp] for b, hp in units]
        yrows = [[] for _ in units]
        for t in range(nsteps):
            row = lambda name, i: blocks[name][i][t:t + 1, :]
            parts = [states[i] * row('a', i) for i in range(nu)]
            parts += [jnp.where(diag, row('v', i), 0.0) for i in range(nu)]
            red = _segsum(jnp.concatenate(parts, axis=0), ones2)
            for i in range(nu):
                sa = red[i * HEAD_DIM:(i + 1) * HEAD_DIM]
                vb = red[(nu + i) * HEAD_DIM:(nu + i + 1) * HEAD_DIM]
                states[i] = states[i] * row('w', i) + sa * row('b', i) + vb * row('k', i)
            yred = _segsum(jnp.concatenate([states[i] * row('r', i) for i in range(nu)], axis=0), ones2)
            for i in range(nu):
                yb = yred[i * HEAD_DIM:(i + 1) * HEAD_DIM]
                yrows[i].append(jnp.sum(jnp.where(diag, yb, 0.0), axis=0, keepdims=True))
        for i, (b, hp) in enumerate(units):
            s_scr[b, hp] = states[i]
            y_ref[b, pl.ds(base, nsteps), hp * LANES:(hp + 1) * LANES] = jnp.concatenate(yrows[i], axis=0)

    if tc % 8 == 0:
        def group(t8, carry):
            run(pl.multiple_of(t8 * 8, 8), 8)
            return carry

        lax.fori_loop(0, tc // 8, group, 0)
    else:
        run(0, tc)

    @pl.when(pl.program_id(0) == pl.num_programs(0) - 1)
    def _():
        sfin_ref[...] = s_scr[...]


def _state_to_pairs(s):
    nb = s.shape[0]
    return s.reshape(nb, 2, 2, HEAD_DIM, HEAD_DIM).transpose(0, 1, 3, 2, 4).reshape(nb, 2, HEAD_DIM, LANES)


def _pairs_to_state(s):
    nb = s.shape[0]
    return s.reshape(nb, 2, HEAD_DIM, 2, HEAD_DIM).transpose(0, 1, 3, 2, 4).reshape(nb, RW_HEADS, HEAD_DIM, HEAD_DIM)


def _rw_scan(r, w, k, v, a, b, s0, *, batch, seq_len, tc):
    args = [t.reshape(batch, seq_len, RW_W) for t in (r, w, k, v, a, b)]
    blk = pl.BlockSpec((batch, tc, RW_W), lambda c: (0, c, 0))
    st = pl.BlockSpec((batch, 2, HEAD_DIM, LANES), lambda c: (0, 0, 0, 0))
    y, sfin = pl.pallas_call(
        _scan_kernel,
        grid=(seq_len // tc,),
        in_specs=[blk] * 6 + [st],
        out_specs=[blk, st],
        out_shape=[jax.ShapeDtypeStruct((batch, seq_len, RW_W), F32),
                   jax.ShapeDtypeStruct((batch, 2, HEAD_DIM, LANES), F32)],
        scratch_shapes=[pltpu.VMEM((batch, 2, HEAD_DIM, LANES), F32)],
        compiler_params=_cparams(("arbitrary",)),
        name="rwkv_scan",
    )(*args, _state_to_pairs(s0))
    return y.reshape(batch * seq_len, RW_W), _pairs_to_state(sfin)


def _mid_kernel(x_ref, ogm_ref, onsa_ref, y_ref, g_ref, bonus_ref, ng_ref, ln_ref, wo_ref,
                wq_ref, kv_ref, wmo_ref, out_ref):
    tm = x_ref.shape[-2]
    rows = max(tm, 8)
    ld = lambda ref: jnp.broadcast_to(ref[...].reshape(tm, ref.shape[-1]), (rows, ref.shape[-1]))
    x = ld(x_ref)
    y = ld(y_ref)
    ym = _head_sum(y) * (1.0 / HEAD_DIM)
    yc = y - ym
    yv = _head_sum(yc * yc) * (1.0 / HEAD_DIM)
    yn = yc * lax.rsqrt(yv + RW_LN_EPS) * ln_ref[0:1, :] + ln_ref[1:2, :]
    o_rw = (yn + ld(bonus_ref)) * ld(g_ref)
    mix = (_dot(ld(ogm_ref), wo_ref[0:GM_W, :]) + _dot(ld(onsa_ref), wo_ref[GM_W:GM_W + NSA_W, :])
           + _dot(o_rw, wo_ref[GM_W + NSA_W:, :]))
    x = x + _rms(mix, ng_ref[1:2, :])
    q = _dot(_rms(x, ng_ref[2:3, :]), wq_ref[...]) * (MEM_HEAD_DIM ** -0.5)
    kv = kv_ref[0]
    heads = []
    for hh in range(MEM_HEADS):
        lo, hi = hh * MEM_HEAD_DIM, (hh + 1) * MEM_HEAD_DIM
        s = _dot_nt(q[:, lo:hi], kv[:, lo:hi])
        e = jnp.exp(s - jnp.max(s, axis=-1, keepdims=True))
        p = e / jnp.sum(e, axis=-1, keepdims=True)
        heads.append(_dot(p, kv[:, D_MODEL + lo:D_MODEL + hi]))
    o = _dot(jnp.concatenate(heads, axis=1), wmo_ref[...])
    x = x + _rms(o, ng_ref[3:4, :])
    out_ref[...] = x[0:tm].reshape(out_ref.shape)


def _mid(x, ogm, onsa, y, g, bonus, kv_mem, P, *, batch, seq_len, tm):
    n = x.shape[0]
    tiles = seq_len // tm
    if tm >= 8:
        row = lambda w: pl.BlockSpec((tm, w), lambda i: (i, 0))
        shp = lambda t: t
        out_shape = jax.ShapeDtypeStruct((n, D_MODEL), F32)
    else:
        row = lambda w: pl.BlockSpec((1, tm, w), lambda i: (i, 0, 0))
        shp = lambda t: t.reshape(n // tm, tm, t.shape[-1])
        out_shape = jax.ShapeDtypeStruct((n // tm, tm, D_MODEL), F32)
    const = lambda shape: pl.BlockSpec(shape, lambda i: (0,) * len(shape))
    out = pl.pallas_call(
        _mid_kernel,
        grid=(n // tm,),
        in_specs=[row(D_MODEL), row(GM_W), row(NSA_W), row(RW_W), row(RW_W), row(RW_W),
                  const((6, D_MODEL)), const((2, RW_W)), const((D_MODEL, D_MODEL)),
                  const((D_MODEL, D_MODEL)),
                  pl.BlockSpec((1, N_MEM, 2 * D_MODEL), lambda i: (i // tiles, 0, 0)),
                  const((D_MODEL, D_MODEL))],
        out_specs=row(D_MODEL),
        out_shape=out_shape,
        compiler_params=_cparams(("parallel",)),
        name="mix_out_mem_attn",
    )(shp(x), shp(ogm), shp(onsa), shp(y), shp(g), shp(bonus), P['norm_g'], P['rw_vec'][4:6],
      P['w_out'].astype(BF16), P['w_mem_q'].astype(BF16), kv_mem, P['w_mem_o'].astype(BF16))
    return out.reshape(n, D_MODEL)


def _memkv_kernel(x_ref, g_ref, w_ref, o_ref, ob_ref):
    o = _dot(_rms(x_ref[...], g_ref[...]), w_ref[...])
    o_ref[...] = o
    ob_ref[...] = o.astype(BF16)


def _mem_kv(mem, g, w_kv, *, tm):
    n = mem.shape[0]
    return pl.pallas_call(
        _memkv_kernel,
        grid=(n // tm,),
        in_specs=[pl.BlockSpec((tm, D_MODEL), lambda i: (i, 0)), pl.BlockSpec((1, D_MODEL), lambda i: (0, 0)),
                  pl.BlockSpec((D_MODEL, 2 * D_MODEL), lambda i: (0, 0))],
        out_specs=[pl.BlockSpec((tm, 2 * D_MODEL), lambda i: (i, 0))] * 2,
        out_shape=[jax.ShapeDtypeStruct((n, 2 * D_MODEL), F32), jax.ShapeDtypeStruct((n, 2 * D_MODEL), BF16)],
        compiler_params=_cparams(("parallel",)),
        name="mem_kv_proj",
    )(mem, g[None, :], w_kv.astype(BF16))


def _ffn_kernel(x_ref, prev_ref, prev1_ref, ng_ref, wg_ref, wu_ref, cw_ref, cb_ref, wo_ref,
                out_ref, tail_ref, h_scr, acc_scr, tail_scr, *, single, tiles_per_seq):
    tm = x_ref.shape[0]
    i, j = pl.program_id(0), pl.program_id(1)

    @pl.when(j == 0)
    def _():
        h_scr[...] = _rms(x_ref[...], ng_ref[4:5, :]).astype(BF16)
        acc_scr[...] = jnp.zeros_like(acc_scr)

    h = h_scr[...]
    gate = jnp.dot(h, wg_ref[...], preferred_element_type=F32)
    up = jnp.dot(h, wu_ref[...], preferred_element_type=F32)
    if single:
        g2 = prev_ref[...]
        g1 = prev1_ref[...]
        tail_ref[...] = gate
    else:
        first = i % tiles_per_seq == 0
        t0 = jnp.where(first, prev_ref[0, 0:1, :], tail_scr[j, 6:7, :])
        t1 = jnp.where(first, prev_ref[0, 1:2, :], tail_scr[j, 7:8, :])
        rowi = lax.broadcasted_iota(jnp.int32, (tm, 1), 0)
        g1 = jnp.where(rowi == 0, t1, pltpu.roll(gate, 1, 0))
        g2 = jnp.where(rowi == 0, t0, jnp.where(rowi == 1, t1, pltpu.roll(gate, 2, 0)))
        tail_scr[j] = gate[tm - 8:tm]
        tail_ref[0] = gate[tm - 2:tm]
    conv = cb_ref[...] + g2 * cw_ref[0:1, :] + g1 * cw_ref[1:2, :] + gate * cw_ref[2:3, :]
    act = jax.nn.silu(conv) * up
    acc_scr[...] += _dot(act, wo_ref[...])

    @pl.when(j == pl.num_programs(1) - 1)
    def _():
        out_ref[...] = x_ref[...] + _rms(acc_scr[...], ng_ref[5:6, :])


def _ffn(x, conv_prev, P, *, seq_len, tm, tf):
    n = x.shape[0]
    single = seq_len == 1
    nj = D_FF // tf
    w_in = P['ffn_w_in'].astype(BF16)
    if single:
        prev_args = (conv_prev[:, 0], conv_prev[:, 1])
        prev_specs = [pl.BlockSpec((tm, tf), lambda i, j: (i, j))] * 2
        tail_spec = pl.BlockSpec((tm, tf), lambda i, j: (i, j))
        tail_shape = jax.ShapeDtypeStruct((n, D_FF), F32)
        tiles = 1
    else:
        tiles = seq_len // tm
        prev_args = (conv_prev, conv_prev)
        prev_specs = [pl.BlockSpec((1, CONV_W - 1, tf), lambda i, j: (i // tiles, 0, j))] * 2
        tail_spec = pl.BlockSpec((1, CONV_W - 1, tf), lambda i, j: (i, 0, j))
        tail_shape = jax.ShapeDtypeStruct((n // tm, CONV_W - 1, D_FF), F32)
    out, tail = pl.pallas_call(
        functools.partial(_ffn_kernel, single=single, tiles_per_seq=tiles),
        grid=(n // tm, nj),
        in_specs=[pl.BlockSpec((tm, D_MODEL), lambda i, j: (i, 0)), *prev_specs,
                  pl.BlockSpec((6, D_MODEL), lambda i, j: (0, 0)),
                  pl.BlockSpec((D_MODEL, tf), lambda i, j: (0, j)),
                  pl.BlockSpec((D_MODEL, tf), lambda i, j: (0, nj + j)),
                  pl.BlockSpec((CONV_W, tf), lambda i, j: (0, j)),
                  pl.BlockSpec((1, tf), lambda i, j: (0, j)),
                  pl.BlockSpec((tf, D_MODEL), lambda i, j: (j, 0))],
        out_specs=[pl.BlockSpec((tm, D_MODEL), lambda i, j: (i, 0)), tail_spec],
        out_shape=[jax.ShapeDtypeStruct((n, D_MODEL), F32), tail_shape],
        scratch_shapes=[pltpu.VMEM((tm, D_MODEL), BF16), pltpu.VMEM((tm, D_MODEL), F32),
                        pltpu.VMEM((nj, 8, tf), F32)],
        compiler_params=_cparams(("arbitrary", "arbitrary")),
        name="conv_ffn",
    )(x, *prev_args, P['norm_g'], w_in, w_in, P['ffn_conv_w'], P['ffn_conv_b'][None, :],
      P['ffn_w_out'].astype(BF16))
    if single:
        tail = jnp.stack([conv_prev[:, 1], tail], axis=1)
    else:
        tail = tail[tiles - 1::tiles]
    return out, tail


def _sample_cmp_win_kernel(pt_ref, qpad_ref, kvc_hbm, win_ref, kvw_ref, oc_ref, ow_ref, idx_ref,
                           kbuf, sem):
    b = pl.program_id(0)
    n_pages = kbuf.shape[0]
    per_page = PAGE_SIZE // CMP_BLOCK

    def page_copy(p):
        return pltpu.make_async_copy(kvc_hbm.at[pl.ds(pt_ref[b, p], 1)], kbuf.at[pl.ds(p, 1)], sem)

    def start(p, c):
        page_copy(p).start()
        return c

    def wait(p, c):
        page_copy(p).wait()
        return c

    lax.fori_loop(0, n_pages, start, 0)
    lax.fori_loop(0, n_pages, wait, 0)
    q8 = jnp.concatenate([qpad_ref[0, :, h * LANES:(h + 1) * LANES].astype(F32) for h in range(NSA_HEADS)],
                         axis=0)
    ss = [_dot_nt(q8, kbuf[:, j * KV_COLS:j * KV_COLS + LANES]) for j in range(per_page)]
    mx = functools.reduce(jnp.maximum, [jnp.max(s, axis=-1, keepdims=True) for s in ss])
    es = [jnp.exp(s - mx) for s in ss]
    den = functools.reduce(lambda x, y: x + y, [jnp.sum(e, axis=-1, keepdims=True) for e in es])
    ps = [e / den for e in es]
    oc = functools.reduce(lambda x, y: x + y,
                          [_dot(ps[j], kbuf[:, j * KV_COLS + LANES:(j + 1) * KV_COLS]) for j in range(per_page)])
    oc_ref[0] = oc
    lane = lax.broadcasted_iota(jnp.int32, (NSA_KV_HEADS, n_pages), 1)
    ns_past = 2 * n_pages
    vals = []
    for jj in range(2):
        pj = ps[2 * jj] + ps[2 * jj + 1]
        rows = [jnp.sum(pj[g * NSA_REP:(g + 1) * NSA_REP], axis=0, keepdims=True) for g in range(NSA_KV_HEADS)]
        val = jnp.concatenate(rows, axis=0)
        n_of = 2 * lane + jj
        vals.append(jnp.where(n_of == 0, FORCE, val))
    cur_val = jnp.full((NSA_KV_HEADS, 1), FORCE, F32)
    big = ns_past + 1
    out_lane = lax.broadcasted_iota(jnp.int32, (NSA_KV_HEADS, LANES), 1)
    idx = jnp.zeros((NSA_KV_HEADS, LANES), jnp.int32)
    for rnd in range(TOP_N):
        mx = jnp.maximum(jnp.maximum(jnp.max(vals[0], axis=-1, keepdims=True),
                                     jnp.max(vals[1], axis=-1, keepdims=True)), cur_val)
        c0 = jnp.min(jnp.where(vals[0] == mx, 2 * lane, big), axis=-1, keepdims=True)
        c1 = jnp.min(jnp.where(vals[1] == mx, 2 * lane + 1, big), axis=-1, keepdims=True)
        c2 = jnp.where(cur_val == mx, ns_past, big)
        pick = jnp.minimum(jnp.minimum(c0, c1), c2)
        idx = jnp.where(out_lane == rnd, pick, idx)
        vals[0] = jnp.where(2 * lane == pick, -jnp.inf, vals[0])
        vals[1] = jnp.where(2 * lane + 1 == pick, -jnp.inf, vals[1])
        cur_val = jnp.where(pick == ns_past, -jnp.inf, cur_val)
    idx_ref[0] = jnp.concatenate([idx, jnp.zeros((8 - NSA_KV_HEADS, LANES), jnp.int32)], axis=0)
    wb = win_ref.shape[1]
    s_old = _dot_nt(q8, win_ref[0, :, 0:LANES])
    new_k = jnp.broadcast_to(kvw_ref[0, :, 0:LANES], (8, LANES))
    s_new = _dot_nt(q8, new_k)[:, 0:1]
    keep = lax.broadcasted_iota(jnp.int32, (1, wb), 1) > wb - WINDOW
    s_old = jnp.where(keep, s_old, NEG)
    mx = jnp.maximum(jnp.max(s_old, axis=-1, keepdims=True), s_new)
    e_old = jnp.exp(s_old - mx)
    e_new = jnp.exp(s_new - mx)
    den = jnp.sum(e_old, axis=-1, keepdims=True) + e_new
    new_v = kvw_ref[0, :, LANES:KV_COLS].astype(BF16).astype(F32)
    ow_ref[0] = _dot(e_old / den, win_ref[0, :, LANES:KV_COLS]) + (e_new / den).astype(BF16).astype(F32) * new_v


def _sample_cmp_win(page_table, qpad, kvc_phys, win_buf, kvw):
    nb, n_pages = page_table.shape
    wb = win_buf.shape[1]
    row_w = (PAGE_SIZE // CMP_BLOCK) * KV_COLS
    piece = jax.ShapeDtypeStruct((nb, NSA_HEADS, LANES), F32)
    grid_spec = pltpu.PrefetchScalarGridSpec(
        num_scalar_prefetch=1,
        grid=(nb,),
        in_specs=[pl.BlockSpec((1, 1, NSA_HEADS * LANES), lambda b, pt: (b, 0, 0)),
                  pl.BlockSpec(memory_space=pl.ANY),
                  pl.BlockSpec((1, wb, KV_COLS), lambda b, pt: (b, 0, 0)),
                  pl.BlockSpec((1, 1, KV_COLS), lambda b, pt: (b, 0, 0))],
        out_specs=[pl.BlockSpec((1, NSA_HEADS, LANES), lambda b, pt: (b, 0, 0))] * 3,
        scratch_shapes=[pltpu.VMEM((n_pages, row_w), F32), pltpu.SemaphoreType.DMA(())],
    )
    return pl.pallas_call(
        _sample_cmp_win_kernel,
        grid_spec=grid_spec,
        out_shape=[piece, piece, jax.ShapeDtypeStruct((nb, 8, LANES), jnp.int32)],
        compiler_params=_cparams(("arbitrary",)),
        name="nsa_sample_cmp_win",
    )(page_table, qpad.reshape(nb, 1, NSA_HEADS * LANES), kvc_phys.reshape(-1, row_w), win_buf,
      kvw.reshape(nb, 1, KV_COLS))


def _sample_sel_kernel(pt_ref, idx_ref, qpad_ref, blk_ref, kvs_ref, gate_ref, oc_ref, ow_ref, out_ref,
                       m_scr, l_scr, acc_scr, *, n_pages):
    b, g, n = pl.program_id(0), pl.program_id(1), pl.program_id(2)

    @pl.when((g == 0) & (n == 0))
    def _():
        m_scr[...] = jnp.full_like(m_scr, NEG)
        l_scr[...] = jnp.zeros_like(l_scr)
        acc_scr[...] = jnp.zeros_like(acc_scr)

    q8 = jnp.concatenate([qpad_ref[0, :, h * LANES:(h + 1) * LANES].astype(F32) for h in range(NSA_HEADS)],
                         axis=0)
    is_cur = idx_ref[b, g * TOP_N + n] >= 2 * n_pages
    blk = jnp.where(is_cur, jnp.broadcast_to(kvs_ref[0], (SLC_BLOCK, KV_COLS)), blk_ref[0])
    s = _dot_nt(q8, blk[:, 0:LANES])
    key = lax.broadcasted_iota(jnp.int32, (NSA_HEADS, SLC_BLOCK), 1)
    row_group = lax.broadcasted_iota(jnp.int32, (NSA_HEADS, SLC_BLOCK), 0) // NSA_REP
    valid = (row_group == g) & (jnp.logical_not(is_cur) | (key == 0))
    s = jnp.where(valid, s, NEG)
    m_old = m_scr[...]
    m_new = jnp.maximum(m_old, jnp.max(s, axis=-1, keepdims=True))
    alpha = jnp.exp(m_old - m_new)
    p = jnp.where(valid, jnp.exp(s - m_new[:, 0:1]), 0.0)
    l_scr[...] = alpha * l_scr[...] + jnp.sum(p, axis=-1, keepdims=True)
    acc_scr[...] = alpha * acc_scr[...] + _dot(p, blk[:, LANES:KV_COLS])
    m_scr[...] = m_new

    @pl.when((g == pl.num_programs(1) - 1) & (n == pl.num_programs(2) - 1))
    def _():
        o_s = acc_scr[...] / l_scr[...]
        head = lax.broadcasted_iota(jnp.int32, (NSA_HEADS, LANES), 0)
        lane = lax.broadcasted_iota(jnp.int32, (NSA_HEADS, LANES), 1)
        gate = jnp.broadcast_to(gate_ref[0], (NSA_HEADS, LANES))
        gcol = lambda br: jnp.sum(jnp.where(lane == br * NSA_HEADS + head, gate, 0.0), axis=-1, keepdims=True)
        o = gcol(0) * oc_ref[0] + gcol(1) * o_s + gcol(2) * ow_ref[0]
        swapped = pltpu.roll(o, HEAD_DIM, 1)
        lane_lo = _lane_lo((1, LANES))
        for c in range(NSA_W // LANES):
            if (2 * c) // NSA_REP == 0:
                col = jnp.where(lane_lo, o[2 * c:2 * c + 1], swapped[2 * c + 1:2 * c + 2])
            else:
                col = jnp.where(lane_lo, swapped[2 * c:2 * c + 1], o[2 * c + 1:2 * c + 2])
            out_ref[0, :, c * LANES:(c + 1) * LANES] = col


def _sample_sel(page_table, idx, qpad, pool_slc, kvs, gates, oc, ow):
    nb, n_pages = page_table.shape
    halves = PAGE_SIZE // SLC_BLOCK
    pool = pool_slc.reshape(-1, SLC_BLOCK, KV_COLS)

    def blk_map(b, g, n, pt, ix):
        blk = ix[b, g * TOP_N + n]
        page = jnp.minimum(blk // halves, n_pages - 1)
        return (pt[b, page] * halves + blk % halves, 0, 0)

    per_b = lambda shape: pl.BlockSpec((1,) + shape, lambda b, g, n, pt, ix: (b, 0, 0))
    grid_spec = pltpu.PrefetchScalarGridSpec(
        num_scalar_prefetch=2,
        grid=(nb, NSA_KV_HEADS, TOP_N),
        in_specs=[per_b((1, NSA_HEADS * LANES)), pl.BlockSpec((1, SLC_BLOCK, KV_COLS), blk_map),
                  per_b((1, KV_COLS)), per_b((1, GATE_PAD)), per_b((NSA_HEADS, LANES)),
                  per_b((NSA_HEADS, LANES))],
        out_specs=per_b((1, NSA_W)),
        scratch_shapes=[pltpu.VMEM((NSA_HEADS, LANES), F32)] * 3,
    )
    out = pl.pallas_call(
        functools.partial(_sample_sel_kernel, n_pages=n_pages),
        grid_spec=grid_spec,
        out_shape=jax.ShapeDtypeStruct((nb, 1, NSA_W), F32),
        compiler_params=_cparams(("arbitrary", "arbitrary", "arbitrary")),
        name="nsa_sample_sel",
    )(page_table, idx, qpad.reshape(nb, 1, NSA_HEADS * LANES), pool, kvs.reshape(nb, 1, KV_COLS),
      gates.reshape(nb, 1, GATE_PAD), oc, ow)
    return out.reshape(nb, NSA_W)


def _tile(n, want):
    t = min(n, want)
    while n % t or (t % 8 and t != n):
        t -= 1
    return t


def _prompt_layer(x, tabs, mem_prompt, P, *, batch, seq_len):
    kv_f32, kv_b16 = _mem_kv(mem_prompt, P['mem_g'], P['w_mem_kv'], tm=_tile(mem_prompt.shape[0], 256))
    w_pack = _pack_w_in(P['w_in'])
    ogm, _, qpad, kvc, kvs, kvw, kvb, gates, zr = _in_proj(
        x, P['norm_g'][0], w_pack, tabs[0], tabs[1], P['gm_ln'], P['gm_ws'], P['gm_bs'],
        seq_len=seq_len, tm=_tile(seq_len, 256))
    pe, wb = _compress_weights(P['cmp_pe'], P['cmp_w'])
    nc = seq_len // CMP_BLOCK
    comp = _compress(kvc.reshape(batch * nc, CMP_K), pe, wb, tr=nc, out_dtype=BF16)
    comp = comp.reshape(batch, nc // 2, 2, KV_COLS).transpose(0, 2, 1, 3).reshape(batch, nc, KV_COLS)
    oc, sel = _cmp_attn(qpad, comp, batch=batch, seq_len=seq_len, tq=_tile(seq_len, 256))
    onsa = _sel_win(qpad, sel, gates, oc, kvb, batch=batch, seq_len=seq_len, tq=128)
    shift0 = jnp.zeros((batch, RW_COLS), F32)
    r, w, k, v, a, b, g, bonus = _rw_prep(zr, shift0, P, seq_len=seq_len, tm=_tile(seq_len, 512))
    s0 = jnp.zeros((batch, RW_HEADS, HEAD_DIM, HEAD_DIM), F32)
    y, s_fin = _rw_scan(r, w, k, v, a, b, s0, batch=batch, seq_len=seq_len, tc=_tile(seq_len, 256))
    x = _mid(x, ogm, onsa, y, g, bonus, kv_b16.reshape(batch, N_MEM, 2 * D_MODEL), P,
             batch=batch, seq_len=seq_len, tm=_tile(seq_len, 256))
    conv0 = jnp.zeros((batch, CONV_W - 1, D_FF), F32)
    x, conv_new = _ffn(x, conv0, P, seq_len=seq_len, tm=_tile(seq_len, 512), tf=256)
    kvshape = (batch, seq_len, 2, NSA_KV_HEADS, HEAD_DIM)
    wbp = min(WINDOW, seq_len)
    states = dict(
        cmp=kvc.reshape(kvshape), slc=kvs.reshape(kvshape), win=kvw.reshape(kvshape)[:, seq_len - wbp:],
        rw=s_fin, shift=zr.reshape(batch, seq_len, RW_COLS)[:, -1], conv=conv_new,
        mem=kv_f32.reshape(batch, N_MEM, 2, MEM_HEADS, MEM_HEAD_DIM))
    return x, states


def _sample_layer(x, tabs, page_table, pool_cmp, pool_slc, win_buf, mem_kv, rw_state, shift_prev,
                  conv_prev, P):
    nb = x.shape[0]
    w_pack = _pack_w_in(P['w_in'])
    ogm, vgm, qpad, kvc, kvs, kvw, _, gates, zr = _in_proj(
        x, P['norm_g'][0], w_pack, tabs[0], tabs[1], P['gm_ln'], P['gm_ws'], P['gm_bs'], seq_len=1, tm=nb)
    pe, wb = _compress_weights(P['cmp_pe'], P['cmp_w'])
    n_phys = pool_cmp.shape[0]
    rows = n_phys * (PAGE_SIZE // CMP_BLOCK)
    kvc_phys = _compress(pool_cmp.reshape(rows, CMP_K), pe, wb, tr=_tile(rows, 256), out_dtype=F32)
    oc, ow, idx = _sample_cmp_win(page_table, qpad, kvc_phys, win_buf.reshape(nb, -1, KV_COLS), kvw)
    idx = idx[:, :NSA_KV_HEADS, :TOP_N].reshape(nb, NSA_KV_HEADS * TOP_N)
    onsa = _sample_sel(page_table, idx, qpad, pool_slc, kvs, gates, oc, ow)
    r, w, k, v, a, b, g, bonus = _rw_prep(zr, shift_prev, P, seq_len=1, tm=nb)
    y, s_fin = _rw_scan(r, w, k, v, a, b, rw_state, batch=nb, seq_len=1, tc=1)
    x = _mid(x, ogm, onsa, y, g, bonus, mem_kv.reshape(nb, N_MEM, 2 * D_MODEL), P, batch=nb, seq_len=1, tm=1)
    x, conv_new = _ffn(x, conv_prev, P, seq_len=1, tm=nb, tf=256)
    kvshape = (nb, 1, 2, NSA_KV_HEADS, HEAD_DIM)
    states = dict(cmp=kvc.reshape(kvshape), slc=kvs.reshape(kvshape), win=kvw.reshape(kvshape), rw=s_fin,
                  shift=zr, conv=conv_new, gv=vgm.reshape(nb, 1, GM_W))
    return x, states


def kernel(x_prompt, x_sample, cache_cmp_kv, cache_slc_kv, cache_win_kv, cache_mem_kv, state_rwkv, state_rwkv_shift, state_ffn_conv, page_table, mem_prompt, norm_g, w_in, w_out, gm_ln, gm_ws, gm_bs, cmp_pe, cmp_w, rw_mu, rw_vec, rw_w2, rw_a2, rw_g2, rw_rk, mem_g, w_mem_q, w_mem_kv, w_mem_o, ffn_w_in, ffn_conv_w, ffn_conv_b, ffn_w_out):
    bp, tp = x_prompt.shape[:2]
    nb, ts = x_sample.shape[:2]
    assert ts == 1 and tp % 512 == 0
    depth = norm_g.shape[0]
    past = page_table.shape[1] * PAGE_SIZE
    tabs_p = _rope_tables(jnp.arange(tp))
    tabs_s = _rope_tables(jnp.full((nb,), past))
    xp = x_prompt.reshape(bp * tp, D_MODEL)
    xs = x_sample.reshape(nb, D_MODEL)
    mem = mem_prompt.reshape(bp * N_MEM, D_MODEL)
    weights = dict(norm_g=norm_g, w_in=w_in, w_out=w_out, gm_ln=gm_ln, gm_ws=gm_ws, gm_bs=gm_bs,
                   cmp_pe=cmp_pe, cmp_w=cmp_w, rw_mu=rw_mu, rw_vec=rw_vec, rw_w2=rw_w2, rw_a2=rw_a2,
                   rw_g2=rw_g2, rw_rk=rw_rk, mem_g=mem_g, w_mem_q=w_mem_q, w_mem_kv=w_mem_kv,
                   w_mem_o=w_mem_o, ffn_w_in=ffn_w_in, ffn_conv_w=ffn_conv_w, ffn_conv_b=ffn_conv_b,
                   ffn_w_out=ffn_w_out)
    ps, ss = [], []
    for l in range(depth):
        P = {name: val[l] for name, val in weights.items()}
        xp, st = _prompt_layer(xp, tabs_p, mem, P, batch=bp, seq_len=tp)
        ps.append(st)
        xs, st = _sample_layer(xs, tabs_s, page_table, cache_cmp_kv[l], cache_slc_kv[l], cache_win_kv[l],
                               cache_mem_kv[l], state_rwkv[l], state_rwkv_shift[l], state_ffn_conv[l], P)
        ss.append(st)
    stack = lambda sts, name: jnp.stack([st[name] for st in sts])
    return (xp.reshape(bp, tp, D_MODEL), xs.reshape(nb, 1, D_MODEL),
            stack(ps, 'cmp'), stack(ps, 'slc'), stack(ps, 'win'), stack(ps, 'rw'), stack(ps, 'shift'),
            stack(ps, 'conv'), stack(ps, 'mem'),
            stack(ss, 'cmp'), stack(ss, 'slc'), stack(ss, 'win'), stack(ss, 'rw'), stack(ss, 'shift'),
            stack(ss, 'conv'), stack(ss, 'gv'))
```

```python
import functools
import math

import jax
import jax.numpy as jnp
from jax import lax
from jax.experimental import pallas as pl
from jax.experimental.pallas import tpu as pltpu

F32 = jnp.float32
BF16 = jnp.bfloat16

D_MODEL = 1024
HEAD_DIM = 64
GM_W = 256
GM_HEADS = 4
CHUNK = 128
NSA_W = 512
NSA_HEADS = 8
NSA_KV_HEADS = 2
NSA_REP = 4
CMP_BLOCK = 32
SLC_BLOCK = 64
TOP_N = 16
WINDOW = 512
RW_W = 256
RW_HEADS = 4
W_LORA = 64
A_LORA = 64
G_LORA = 128
RW_COLS = 3 * RW_W + W_LORA + A_LORA + G_LORA
RW_LN_EPS = 64e-5
N_MEM = 256
MEM_HEADS = 4
MEM_HEAD_DIM = 256
D_FF = 2816
CONV_W = 3
PAGE_SIZE = 128
ROPE_THETA = 10000.0
NORM_EPS = 1e-6
LN_EPS = 1e-5
NEG = -1e30
FORCE = 1e4
OFF_GM = 0
OFF_Q = 512
OFF_KV = 1024
OFF_GATE = 1792
OFF_RW = 1816
IN_COLS = 2840
KV_COLS = 2 * NSA_KV_HEADS * HEAD_DIM
CMP_K = CMP_BLOCK * KV_COLS
GATE_PAD = 128
PACK_COLS = OFF_GATE + RW_COLS + GATE_PAD

LANES = 128
VMEM_LIMIT = 56 * 1024 * 1024


def _cparams(sem):
    return pltpu.CompilerParams(dimension_semantics=sem, vmem_limit_bytes=VMEM_LIMIT)


def _rms(x, g):
    return x * lax.rsqrt(jnp.mean(x * x, axis=-1, keepdims=True) + NORM_EPS) * g


def _dot(a, b):
    return jnp.dot(a.astype(BF16), b.astype(BF16), preferred_element_type=F32)


def _dot_nt(a, b):
    return lax.dot_general(a.astype(BF16), b.astype(BF16), (((1,), (1,)), ((), ())),
                           preferred_element_type=F32)


def _lane_lo(shape):
    return lax.broadcasted_iota(jnp.int32, shape, len(shape) - 1) % LANES < HEAD_DIM


def _head_sum(x):
    cols = []
    for c in range(x.shape[1] // LANES):
        xc = x[:, c * LANES:(c + 1) * LANES]
        lo_mask = _lane_lo(xc.shape)
        lo = jnp.sum(jnp.where(lo_mask, xc, 0.0), axis=-1, keepdims=True)
        hi = jnp.sum(jnp.where(lo_mask, 0.0, xc), axis=-1, keepdims=True)
        cols.append(jnp.where(lo_mask, lo, hi))
    return jnp.concatenate(cols, axis=1)


def _inproj_kernel(x_ref, g_ref, w_ref, cos_ref, sin_ref, ln_ref, ws_ref, bias_ref,
                   ogm_ref, vgm_ref, qpad_ref, kvc_ref, kvs_ref, kvw_ref, kvb_ref, gate_ref, zr_ref, *q_ref,
                   single):
    tm = x_ref.shape[0]
    h = _rms(x_ref[...], g_ref[...])
    z = _dot(h, w_ref[...])
    zg = jax.nn.gelu(z[:, 0:2 * GM_W])
    u = zg[:, :GM_W]
    vv = zg[:, GM_W:]
    mean = jnp.mean(vv, axis=-1, keepdims=True)
    var = jnp.mean(jnp.square(vv - mean), axis=-1, keepdims=True)
    v = (vv - mean) * lax.rsqrt(var + LN_EPS) * ln_ref[0:1, :] + ln_ref[1:2, :]
    vgm_ref[...] = v
    if single:
        ogm_ref[...] = (u * (v * ws_ref[...] + bias_ref[...])).astype(BF16)
    else:
        head_of_lane = lax.broadcasted_iota(jnp.int32, (CHUNK, GM_W), 1) // HEAD_DIM
        causal = (lax.broadcasted_iota(jnp.int32, (CHUNK, CHUNK), 0)
                  >= lax.broadcasted_iota(jnp.int32, (CHUNK, CHUNK), 1))
        wms = [jnp.where(causal, ws_ref[hh], 0.0).astype(BF16) for hh in range(GM_HEADS)]
        for c in range(tm // CHUNK):
            vc = v[c * CHUNK:(c + 1) * CHUNK]
            acc = bias_ref[...]
            for hh in range(GM_HEADS):
                vm = jnp.where(head_of_lane == hh, vc, 0.0).astype(BF16)
                acc = acc + jnp.dot(wms[hh], vm, preferred_element_type=F32)
            ogm_ref[c * CHUNK:(c + 1) * CHUNK, :] = (u[c * CHUNK:(c + 1) * CHUNK] * acc).astype(BF16)
    cos = cos_ref[...]
    sin = sin_ref[...]
    first_half = lax.broadcasted_iota(jnp.int32, (tm, LANES), 1) % HEAD_DIM < HEAD_DIM // 2
    lane_lo = _lane_lo((tm, LANES))

    def rope(xc):
        rot = jnp.where(first_half, pltpu.roll(xc, LANES - HEAD_DIM // 2, 1), pltpu.roll(xc, HEAD_DIM // 2, 1))
        return xc * cos + rot * sin

    scale = HEAD_DIM ** -0.5
    for c in range(NSA_W // LANES):
        qc = rope(z[:, OFF_Q + c * LANES:OFF_Q + (c + 1) * LANES]) * scale
        if single:
            q_ref[0][:, c * LANES:(c + 1) * LANES] = qc
        qr = pltpu.roll(qc, HEAD_DIM, 1)
        if (2 * c) // NSA_REP == 0:
            even = jnp.where(lane_lo, qc, 0.0)
            odd = jnp.where(lane_lo, qr, 0.0)
        else:
            even = jnp.where(lane_lo, 0.0, qr)
            odd = jnp.where(lane_lo, 0.0, qc)
        qpad_ref[:, (2 * c) * LANES:(2 * c + 1) * LANES] = even.astype(BF16)
        qpad_ref[:, (2 * c + 1) * LANES:(2 * c + 2) * LANES] = odd.astype(BF16)
    for j, ref in enumerate((kvc_ref, kvs_ref, kvw_ref)):
        base = OFF_KV + j * KV_COLS
        kk = rope(z[:, base:base + LANES])
        vj = z[:, base + LANES:base + KV_COLS]
        ref[:, 0:LANES] = kk
        ref[:, LANES:KV_COLS] = vj
        if j >= 1:
            kvb_ref[:, (j - 1) * KV_COLS:(j - 1) * KV_COLS + LANES] = kk.astype(BF16)
            kvb_ref[:, (j - 1) * KV_COLS + LANES:j * KV_COLS] = vj.astype(BF16)
    zr_ref[...] = z[:, OFF_GATE:OFF_GATE + RW_COLS]
    gate_ref[...] = jax.nn.sigmoid(z[:, OFF_GATE + RW_COLS:])


def _pack_w_in(w_in):
    gate = w_in[:, OFF_GATE:OFF_RW].reshape(D_MODEL, NSA_HEADS, 3)
    gate = jnp.transpose(gate, (0, 2, 1)).reshape(D_MODEL, 3 * NSA_HEADS)
    gate = jnp.pad(gate, ((0, 0), (0, GATE_PAD - 3 * NSA_HEADS)))
    return jnp.concatenate([w_in[:, :OFF_GATE], w_in[:, OFF_RW:], gate], axis=1).astype(BF16)


def _in_proj(x, g0, w_pack, cos, sin, gm_ln, gm_ws, gm_bs, *, seq_len, tm):
    n = x.shape[0]
    single = seq_len == 1
    if single:
        ws_arg = jnp.repeat(gm_ws[:, 0, 0], HEAD_DIM)[None, :]
        bias_arg = jnp.repeat(gm_bs[:, 0], HEAD_DIM)[None, :]
        ws_spec = pl.BlockSpec((1, GM_W), lambda i: (0, 0))
        bias_spec = pl.BlockSpec((1, GM_W), lambda i: (0, 0))
        tab_map = lambda i: (i, 0)
    else:
        assert seq_len % tm == 0 and tm % CHUNK == 0
        ws_arg = gm_ws
        bias_arg = jnp.repeat(gm_bs.T, HEAD_DIM, axis=1)
        ws_spec = pl.BlockSpec((GM_HEADS, CHUNK, CHUNK), lambda i: (0, 0, 0))
        bias_spec = pl.BlockSpec((CHUNK, GM_W), lambda i: (0, 0))
        tiles = seq_len // tm
        tab_map = lambda i: (i % tiles, 0)
    row = lambda w: pl.BlockSpec((tm, w), lambda i: (i, 0))
    outs = [(GM_W, BF16), (GM_W, F32), (NSA_HEADS * LANES, BF16), (KV_COLS, F32), (KV_COLS, F32),
            (KV_COLS, F32), (2 * KV_COLS, BF16), (GATE_PAD, F32), (RW_COLS, F32)]
    if single:
        outs.append((NSA_W, F32))
    return pl.pallas_call(
        functools.partial(_inproj_kernel, single=single),
        grid=(n // tm,),
        in_specs=[row(D_MODEL), pl.BlockSpec((1, D_MODEL), lambda i: (0, 0)),
                  pl.BlockSpec((D_MODEL, PACK_COLS), lambda i: (0, 0)),
                  pl.BlockSpec((tm, LANES), tab_map), pl.BlockSpec((tm, LANES), tab_map),
                  pl.BlockSpec((2, GM_W), lambda i: (0, 0)), ws_spec, bias_spec],
        out_specs=[row(w) for w, _ in outs],
        out_shape=[jax.ShapeDtypeStruct((n, w), dt) for w, dt in outs],
        compiler_params=_cparams(("parallel",)),
        name="in_proj",
    )(x, g0[None, :], w_pack, cos, sin, gm_ln, ws_arg, bias_arg)


def _rope_tables(pos):
    half = HEAD_DIM // 2
    inv = ROPE_THETA ** (-jnp.arange(half, dtype=F32) / half)
    ang = pos.astype(F32)[:, None] * inv[None, :]
    cos, sin = jnp.cos(ang), jnp.sin(ang)
    return jnp.tile(cos, (1, 4)), jnp.tile(jnp.concatenate([-sin, sin], axis=1), (1, 2))


def _compress_rows(k_rows_ref, v_rows_ref, pe_ref, w_ref, n_out):
    acc_k = jnp.zeros((n_out, LANES), F32)
    acc_v = jnp.zeros((n_out, LANES), F32)
    for l in range(CMP_BLOCK):
        xk = k_rows_ref[pl.ds(l, n_out, stride=CMP_BLOCK), :] + pe_ref[l:l + 1, 0:LANES]
        xv = v_rows_ref[pl.ds(l, n_out, stride=CMP_BLOCK), :] + pe_ref[l:l + 1, LANES:KV_COLS]
        acc_k = acc_k + _dot(xk, w_ref[l, 0:LANES, 0:LANES])
        acc_v = acc_v + _dot(xv, w_ref[l, LANES:KV_COLS, LANES:KV_COLS])
    return jnp.concatenate([acc_k, acc_v], axis=1)


def _compress_kernel(k_ref, v_ref, pe_ref, w_ref, o_ref):
    o_ref[...] = _compress_rows(k_ref, v_ref, pe_ref, w_ref, o_ref.shape[0]).astype(o_ref.dtype)


def _compress_pages_kernel(x_ref, pe_ref, w_ref, o_ref, k_scr, v_scr):
    def transpose_page(p, c):
        rows = pl.ds(pl.multiple_of(p * PAGE_SIZE, PAGE_SIZE), PAGE_SIZE)
        k_scr[rows, :] = x_ref[0, p, 0:LANES, :].T
        v_scr[rows, :] = x_ref[0, p, LANES:KV_COLS, :].T
        return c

    lax.fori_loop(0, x_ref.shape[1], transpose_page, 0)
    o_ref[...] = _compress_rows(k_scr, v_scr, pe_ref, w_ref, o_ref.shape[0])


def _compress_weights(cmp_pe, cmp_w):
    eye = jnp.eye(2 * NSA_KV_HEADS, dtype=F32).reshape(2, NSA_KV_HEADS, 2, NSA_KV_HEADS)
    wb = jnp.einsum('lsde,sgtq->lsgdtqe', cmp_w, eye).reshape(CMP_BLOCK, KV_COLS, KV_COLS).astype(BF16)
    pe = jnp.broadcast_to(cmp_pe[:, :, None, :], (CMP_BLOCK, 2, NSA_KV_HEADS, HEAD_DIM)).reshape(CMP_BLOCK, KV_COLS)
    return pe, wb


def _cmp_const_specs():
    return [pl.BlockSpec((CMP_BLOCK, KV_COLS), lambda i: (0, 0)),
            pl.BlockSpec((CMP_BLOCK, KV_COLS, KV_COLS), lambda i: (0, 0, 0))]


def _compress(x, pe, wb, *, tr):
    n = x.shape[0]
    return pl.pallas_call(
        _compress_kernel,
        grid=(n // tr,),
        in_specs=[pl.BlockSpec((tr, LANES), lambda i: (i, 0)), pl.BlockSpec((tr, LANES), lambda i: (i, 1))]
        + _cmp_const_specs(),
        out_specs=pl.BlockSpec((tr // CMP_BLOCK, KV_COLS), lambda i: (i, 0)),
        out_shape=jax.ShapeDtypeStruct((n // CMP_BLOCK, KV_COLS), BF16),
        compiler_params=_cparams(("parallel",)),
        name="nsa_compress",
    )(x, x, pe, wb)


def _pool_pages(pool):
    depth, n_phys = pool.shape[:2]
    return jnp.transpose(pool, (0, 1, 3, 4, 5, 2)).reshape(depth, n_phys, KV_COLS, PAGE_SIZE)


def _compress_pages(pool_pages, layer, pe, wb, *, pages):
    n_phys = pool_pages.shape[1]
    per_page = PAGE_SIZE // CMP_BLOCK
    return pl.pallas_call(
        _compress_pages_kernel,
        grid=(n_phys // pages,),
        in_specs=[pl.BlockSpec((1, pages, KV_COLS, PAGE_SIZE), lambda i: (layer, i, 0, 0))] + _cmp_const_specs(),
        out_specs=pl.BlockSpec((pages * per_page, KV_COLS), lambda i: (i, 0)),
        out_shape=jax.ShapeDtypeStruct((n_phys * per_page, KV_COLS), F32),
        scratch_shapes=[pltpu.VMEM((pages * PAGE_SIZE, LANES), F32)] * 2,
        compiler_params=_cparams(("parallel",)),
        name="nsa_compress_pages",
    )(pool_pages, pe, wb)


def _group_q(qpad_ref, g):
    return jnp.concatenate([qpad_ref[:, (g * NSA_REP + r) * LANES:(g * NSA_REP + r + 1) * LANES]
                            for r in range(NSA_REP)], axis=0)


def _assemble_heads(pieces, tq):
    lane_lo = _lane_lo((tq, LANES))
    cols = []
    for c in range(NSA_W // LANES):
        g = (2 * c) // NSA_REP
        r = (2 * c) % NSA_REP
        even = pieces[g][r * tq:(r + 1) * tq]
        odd = pieces[g][(r + 1) * tq:(r + 2) * tq]
        if g == 0:
            cols.append(jnp.where(lane_lo, even, pltpu.roll(odd, HEAD_DIM, 1)))
        else:
            cols.append(jnp.where(lane_lo, pltpu.roll(even, HEAD_DIM, 1), odd))
    return cols


def _cmp_attn_kernel(qpad_ref, kvc_ref, oc_ref, sel_ref):
    tq = qpad_ref.shape[0]
    q0 = pl.program_id(1) * tq
    kvc = kvc_ref[0]
    nc = kvc.shape[0]
    ns = nc // 2
    coli = lax.broadcasted_iota(jnp.int32, (1, nc), 1)
    cblk = jnp.where(coli < ns, 2 * coli, 2 * (coli - ns) + 1)
    c_end = cblk * CMP_BLOCK + (CMP_BLOCK - 1)
    rowpos = q0 + lax.broadcasted_iota(jnp.int32, (NSA_REP * tq, 1), 0) % tq
    m_c = c_end <= rowpos
    blk = lax.broadcasted_iota(jnp.int32, (ns, tq), 0)
    cur = (q0 + lax.broadcasted_iota(jnp.int32, (ns, tq), 1)) // SLC_BLOCK
    pieces = []
    for g in range(NSA_KV_HEADS):
        qg = _group_q(qpad_ref, g)
        s = _dot_nt(qg, kvc[:, 0:LANES])
        sm = jnp.where(m_c, s, NEG)
        e = jnp.exp(sm - jnp.max(sm, axis=-1, keepdims=True))
        p = jnp.where(m_c, e / jnp.sum(e, axis=-1, keepdims=True), 0.0)
        pieces.append(_dot(p, kvc[:, LANES:KV_COLS]))
        ps = p[0:tq] + p[tq:2 * tq] + p[2 * tq:3 * tq] + p[3 * tq:4 * tq]
        imp = (ps[:, :ns] + ps[:, ns:]).T
        imp = jnp.where((blk == cur) | (blk == 0), FORCE, imp)
        imp = jnp.where(blk > cur, NEG, imp)
        work = imp
        sel = jnp.zeros((ns, tq), F32)
        for _ in range(min(TOP_N, ns)):
            mx = jnp.max(work, axis=0, keepdims=True)
            first = jnp.min(jnp.where(work == mx, blk, ns), axis=0, keepdims=True)
            pick = blk == first
            sel = jnp.where(pick, 1.0, sel)
            work = jnp.where(pick, -jnp.inf, work)
        sel = jnp.where(imp > 0.5 * NEG, sel, 0.0)
        sel_ref[:, g * ns:(g + 1) * ns] = sel.T.astype(BF16)
    cols = _assemble_heads(pieces, tq)
    for c in range(NSA_W // LANES):
        oc_ref[:, c * LANES:(c + 1) * LANES] = cols[c]


def _cmp_attn(qpad, kvc_perm, *, batch, seq_len, tq):
    n = qpad.shape[0]
    nq = seq_len // tq
    nc = kvc_perm.shape[1]
    return pl.pallas_call(
        _cmp_attn_kernel,
        grid=(batch, nq),
        in_specs=[pl.BlockSpec((tq, NSA_HEADS * LANES), lambda b, i: (b * nq + i, 0)),
                  pl.BlockSpec((1, nc, KV_COLS), lambda b, i: (b, 0, 0))],
        out_specs=[pl.BlockSpec((tq, NSA_W), lambda b, i: (b * nq + i, 0)),
                   pl.BlockSpec((tq, nc), lambda b, i: (b * nq + i, 0))],
        out_shape=[jax.ShapeDtypeStruct((n, NSA_W), F32), jax.ShapeDtypeStruct((n, nc), BF16)],
        compiler_params=_cparams(("parallel", "parallel")),
        name="nsa_cmp_attn",
    )(qpad, kvc_perm)


SEL_KV_TILE = 512


def _gate_col(gate, br, c, tq):
    lane_lo = _lane_lo((tq, LANES))
    i0 = br * NSA_HEADS + 2 * c
    return jnp.where(lane_lo, gate[:, i0:i0 + 1], gate[:, i0 + 1:i0 + 2])


def _sel_win_kernel(qpad_ref, sel_ref, gate_ref, oc_ref, kvb_ref, out_ref, *, seq_len):
    tq = qpad_ref.shape[0]
    rows = NSA_REP * tq
    kt = min(SEL_KV_TILE, seq_len)
    ns = seq_len // SLC_BLOCK
    q0 = pl.program_id(1) * tq
    assert kt % tq == 0
    j_diag = q0 // kt
    key_blk = lax.broadcasted_iota(jnp.int32, (kt, ns), 0) // SLC_BLOCK
    blk_lane = lax.broadcasted_iota(jnp.int32, (kt, ns), 1)
    lane_kt = lax.broadcasted_iota(jnp.int32, (1, kt), 1)
    n_win = (WINDOW + tq) // LANES
    all_rows = NSA_KV_HEADS * rows
    rowpos = q0 + lax.broadcasted_iota(jnp.int32, (all_rows, 1), 0) % tq
    qs, unsels = [], []
    for g in range(NSA_KV_HEADS):
        unsel = (1.0 - sel_ref[:, g * ns:(g + 1) * ns].astype(F32)).astype(BF16)
        qs.append(_group_q(qpad_ref, g))
        unsels.extend([unsel] * NSA_REP)
    q_all = jnp.concatenate(qs, axis=0)
    lhs = jnp.concatenate([q_all, jnp.concatenate(unsels, axis=0)], axis=1)
    ones = jnp.ones((1, LANES), BF16)

    def with_ones(v):
        return jnp.concatenate([v, jnp.broadcast_to(ones, v.shape)], axis=1)

    def tile(j, carry, diagonal):
        m, acc = carry
        start = pl.multiple_of(j * kt, kt)
        k = kvb_ref[0, pl.ds(start, kt), 0:LANES]
        v = kvb_ref[0, pl.ds(start, kt), LANES:KV_COLS]
        penalty = jnp.where(blk_lane == key_blk + j * (kt // SLC_BLOCK), NEG, 0.0).astype(BF16)
        s = _dot_nt(lhs, jnp.concatenate([k, penalty], axis=1))
        if diagonal:
            s = jnp.where(start + lane_kt <= rowpos, s, NEG)
        m_new = jnp.maximum(m, jnp.max(s, axis=-1, keepdims=True))
        p = jnp.exp(s - m_new).astype(BF16)
        acc = jnp.exp(m - m_new) * acc + jnp.dot(p, with_ones(v), preferred_element_type=F32)
        return m_new, acc

    init = (jnp.full((all_rows, 1), NEG, F32), jnp.zeros((all_rows, 2 * LANES), F32))
    carry = lax.fori_loop(0, j_diag, functools.partial(tile, diagonal=False), init)
    _, acc = tile(j_diag, carry, True)
    o_sel = acc[:, 0:LANES] / acc[:, LANES:2 * LANES]
    ss, vs = [], []
    for j in range(n_win):
        start_true = q0 - WINDOW + j * LANES
        start = pl.multiple_of(jnp.maximum(start_true, 0), LANES)
        ss.append(_dot_nt(q_all, kvb_ref[0, pl.ds(start, LANES), KV_COLS:KV_COLS + LANES]))
        vs.append(kvb_ref[0, pl.ds(start, LANES), KV_COLS + LANES:2 * KV_COLS])
    kpos = q0 - WINDOW + lax.broadcasted_iota(jnp.int32, (1, n_win * LANES), 1)
    valid = (kpos <= rowpos) & (kpos > rowpos - WINDOW) & (kpos >= 0)
    s = jnp.where(valid, jnp.concatenate(ss, axis=1), NEG)
    e = jnp.where(valid, jnp.exp(s - jnp.max(s, axis=-1, keepdims=True)), 0.0).astype(BF16)
    acc = jnp.dot(e, with_ones(jnp.concatenate(vs, axis=0)), preferred_element_type=F32)
    o_win = acc[:, 0:LANES] / acc[:, LANES:2 * LANES]
    sel_pieces = [o_sel[g * rows:(g + 1) * rows] for g in range(NSA_KV_HEADS)]
    win_pieces = [o_win[g * rows:(g + 1) * rows] for g in range(NSA_KV_HEADS)]
    sel_cols = _assemble_heads(sel_pieces, tq)
    win_cols = _assemble_heads(win_pieces, tq)
    gate = gate_ref[...]
    for c in range(NSA_W // LANES):
        oc = oc_ref[:, c * LANES:(c + 1) * LANES]
        o = (_gate_col(gate, 0, c, tq) * oc + _gate_col(gate, 1, c, tq) * sel_cols[c]
             + _gate_col(gate, 2, c, tq) * win_cols[c])
        out_ref[:, c * LANES:(c + 1) * LANES] = o.astype(BF16)


def _sel_win(qpad, sel, gates, oc, kvb, *, batch, seq_len, tq):
    n = qpad.shape[0]
    nq = seq_len // tq
    ns2 = sel.shape[1]
    row = lambda w: pl.BlockSpec((tq, w), lambda b, i: (b * nq + i, 0))
    return pl.pallas_call(
        functools.partial(_sel_win_kernel, seq_len=seq_len),
        grid=(batch, nq),
        in_specs=[row(NSA_HEADS * LANES), row(ns2), row(GATE_PAD), row(NSA_W),
                  pl.BlockSpec((1, seq_len, 2 * KV_COLS), lambda b, i: (b, 0, 0))],
        out_specs=row(NSA_W),
        out_shape=jax.ShapeDtypeStruct((n, NSA_W), BF16),
        compiler_params=_cparams(("parallel", "parallel")),
        name="nsa_sel_win",
    )(qpad, sel, gates, oc, kvb.reshape(batch, seq_len, 2 * KV_COLS))


def _rwprep_kernel(zr_ref, prev_ref, halo_ref, mu_ref, vec_ref, w2_ref, a2_ref, g2_ref, rk_ref,
                   r_o, w_o, k_o, v_o, a_o, b_o, g_o, bonus_o, *, single, tiles_per_seq):
    tm = zr_ref.shape[0]
    zr = zr_ref[...]
    if single:
        prev = prev_ref[...]
    else:
        first_tile = pl.program_id(0) % tiles_per_seq == 0
        before = jnp.where(first_tile, prev_ref[0], halo_ref[7:8, :])
        rowi = lax.broadcasted_iota(jnp.int32, (tm, 1), 0)
        prev = jnp.where(rowi == 0, before, pltpu.roll(zr, 1, 0))
    zs = zr + (prev - zr) * mu_ref[...]
    r = zs[:, 0:RW_W]
    k = zs[:, RW_W:2 * RW_W]
    v = zs[:, 2 * RW_W:3 * RW_W]
    lora = zs[:, 3 * RW_W:3 * RW_W + W_LORA + A_LORA]
    w0, a0, k_k, k_a = (vec_ref[i:i + 1, :] for i in range(4))
    wlog = -jax.nn.softplus(-(w0 + _dot(jnp.tanh(lora), w2_ref[...]))) - 0.5
    decay = jnp.exp(-jnp.exp(wlog))
    a = jax.nn.sigmoid(a0 + _dot(lora, a2_ref[...]))
    g = _dot(jax.nn.sigmoid(zs[:, 3 * RW_W + W_LORA + A_LORA:]), g2_ref[...])
    kk = k * k_k
    kk = kk / jnp.maximum(jnp.sqrt(_head_sum(kk * kk)), 1e-12)
    k2 = k * (1.0 + (a - 1.0) * k_a)
    r_o[...] = r
    w_o[...] = decay
    k_o[...] = k2
    v_o[...] = v
    a_o[...] = -kk
    b_o[...] = kk * a
    g_o[...] = g
    bonus_o[...] = _head_sum(r * k2 * rk_ref[...]) * v


def _rw_prep(zr, shift_prev, P, *, seq_len, tm):
    n = zr.shape[0]
    single = seq_len == 1
    w2p = jnp.concatenate([P['rw_w2'], jnp.zeros((A_LORA, RW_W), F32)], axis=0).astype(BF16)
    a2p = jnp.concatenate([jnp.zeros((W_LORA, RW_W), F32), P['rw_a2']], axis=0).astype(BF16)
    const = lambda shape: pl.BlockSpec(shape, lambda i: (0,) * len(shape))
    if single:
        prev_arg = shift_prev
        prev_spec = pl.BlockSpec((tm, RW_COLS), lambda i: (i, 0))
        tiles = 1
    else:
        tiles = seq_len // tm
        prev_arg = shift_prev[:, None, :]
        prev_spec = pl.BlockSpec((1, 1, RW_COLS), lambda i: (i // tiles, 0, 0))
    halo_spec = pl.BlockSpec((8, RW_COLS), lambda i: (jnp.maximum(i * (tm // 8) - 1, 0), 0))
    row = pl.BlockSpec((tm, RW_W), lambda i: (i, 0))
    return pl.pallas_call(
        functools.partial(_rwprep_kernel, single=single, tiles_per_seq=tiles),
        grid=(n // tm,),
        in_specs=[pl.BlockSpec((tm, RW_COLS), lambda i: (i, 0)), prev_spec, halo_spec,
                  const((1, RW_COLS)), const((6, RW_W)), const((W_LORA + A_LORA, RW_W)),
                  const((W_LORA + A_LORA, RW_W)), const((G_LORA, RW_W)), const((1, RW_W))],
        out_specs=[row] * 8,
        out_shape=[jax.ShapeDtypeStruct((n, RW_W), F32)] * 8,
        compiler_params=_cparams(("parallel",)),
        name="rwkv_prep",
    )(zr, prev_arg, zr, P['rw_mu'][None, :], P['rw_vec'], w2p, a2p, P['rw_g2'].astype(BF16),
      P['rw_rk'].reshape(1, RW_W))


def _segsum(x, ones2):
    hi = x.astype(BF16)
    lo = (x - hi.astype(F32)).astype(BF16)
    return jnp.dot(jnp.concatenate([hi, lo], axis=1), ones2, preferred_element_type=F32)


def _scan_kernel(r_ref, w_ref, k_ref, v_ref, a_ref, b_ref, s0_ref, y_ref, sfin_ref, s_scr):
    nb, tc = r_ref.shape[0], r_ref.shape[1]
    npair = RW_HEADS // 2
    units = [(b, hp) for b in range(nb) for hp in range(npair)]

    @pl.when(pl.program_id(0) == 0)
    def _():
        s_scr[...] = s0_ref[...]

    ri = lax.broadcasted_iota(jnp.int32, (2 * LANES, LANES), 0) % LANES
    ci = lax.broadcasted_iota(jnp.int32, (2 * LANES, LANES), 1)
    ones2 = (ri // HEAD_DIM == ci // HEAD_DIM).astype(BF16)
    diag = (lax.broadcasted_iota(jnp.int32, (HEAD_DIM, LANES), 0)
            == lax.broadcasted_iota(jnp.int32, (HEAD_DIM, LANES), 1) % HEAD_DIM)

    nu = len(units)
    refs = dict(r=r_ref, w=w_ref, k=k_ref, v=v_ref, a=a_ref, b=b_ref)

    def run(base, nsteps):
        blocks = {name: [ref[b, pl.ds(base, nsteps), hp * LANES:(hp + 1) * LANES] for b, hp in units]
                  for name, ref in refs.items()}
        states = [s_scr[b, hp] for b, hp in units]
        yrows = [[] for _ in units]
        for t in range(nsteps):
            row = lambda name, i: blocks[name][i][t:t + 1, :]
            parts = [states[i] * row('a', i) for i in range(nu)]
            parts += [jnp.where(diag, row('v', i), 0.0) for i in range(nu)]
            red = _segsum(jnp.concatenate(parts, axis=0), ones2)
            for i in range(nu):
                sa = red[i * HEAD_DIM:(i + 1) * HEAD_DIM]
                vb = red[(nu + i) * HEAD_DIM:(nu + i + 1) * HEAD_DIM]
                states[i] = states[i] * row('w', i) + sa * row('b', i) + vb * row('k', i)
            yred = _segsum(jnp.concatenate([states[i] * row('r', i) for i in range(nu)], axis=0), ones2)
            for i in range(nu):
                yb = yred[i * HEAD_DIM:(i + 1) * HEAD_DIM]
                yrows[i].append(jnp.sum(jnp.where(diag, yb, 0.0), axis=0, keepdims=True))
        for i, (b, hp) in enumerate(units):
            s_scr[b, hp] = states[i]
            y_ref[b, pl.ds(base, nsteps), hp * LANES:(hp + 1) * LANES] = jnp.concatenate(yrows[i], axis=0)

    if tc % 8 == 0:
        def group(t8, carry):
            run(pl.multiple_of(t8 * 8, 8), 8)
            return carry

        lax.fori_loop(0, tc // 8, group, 0)
    else:
        run(0, tc)

    @pl.when(pl.program_id(0) == pl.num_programs(0) - 1)
    def _():
        sfin_ref[...] = s_scr[...]


def _state_to_pairs(s):
    nb = s.shape[0]
    return s.reshape(nb, 2, 2, HEAD_DIM, HEAD_DIM).transpose(0, 1, 3, 2, 4).reshape(nb, 2, HEAD_DIM, LANES)


def _pairs_to_state(s):
    nb = s.shape[0]
    return s.reshape(nb, 2, HEAD_DIM, 2, HEAD_DIM).transpose(0, 1, 3, 2, 4).reshape(nb, RW_HEADS, HEAD_DIM, HEAD_DIM)


def _rw_scan(r, w, k, v, a, b, s0, *, batch, seq_len, tc):
    args = [t.reshape(batch, seq_len, RW_W) for t in (r, w, k, v, a, b)]
    blk = pl.BlockSpec((batch, tc, RW_W), lambda c: (0, c, 0))
    st = pl.BlockSpec((batch, 2, HEAD_DIM, LANES), lambda c: (0, 0, 0, 0))
    y, sfin = pl.pallas_call(
        _scan_kernel,
        grid=(seq_len // tc,),
        in_specs=[blk] * 6 + [st],
        out_specs=[blk, st],
        out_shape=[jax.ShapeDtypeStruct((batch, seq_len, RW_W), F32),
                   jax.ShapeDtypeStruct((batch, 2, HEAD_DIM, LANES), F32)],
        scratch_shapes=[pltpu.VMEM((batch, 2, HEAD_DIM, LANES), F32)],
        compiler_params=_cparams(("arbitrary",)),
        name="rwkv_scan",
    )(*args, _state_to_pairs(s0))
    return y.reshape(batch * seq_len, RW_W), _pairs_to_state(sfin)


def _mid_kernel(x_ref, ogm_ref, onsa_ref, y_ref, g_ref, bonus_ref, ng_ref, ln_ref, wo_ref,
                wq_ref, kv_ref, wmo_ref, out_ref):
    tm = x_ref.shape[-2]
    rows = max(tm, 8)
    ld = lambda ref: jnp.broadcast_to(ref[...].reshape(tm, ref.shape[-1]), (rows, ref.shape[-1]))
    x = ld(x_ref)
    y = ld(y_ref)
    ym = _head_sum(y) * (1.0 / HEAD_DIM)
    yc = y - ym
    yv = _head_sum(yc * yc) * (1.0 / HEAD_DIM)
    yn = yc * lax.rsqrt(yv + RW_LN_EPS) * ln_ref[0:1, :] + ln_ref[1:2, :]
    o_rw = (yn + ld(bonus_ref)) * ld(g_ref)
    mix = (_dot(ld(ogm_ref), wo_ref[0:GM_W, :]) + _dot(ld(onsa_ref), wo_ref[GM_W:GM_W + NSA_W, :])
           + _dot(o_rw, wo_ref[GM_W + NSA_W:, :]))
    x = x + _rms(mix, ng_ref[1:2, :])
    q = _dot(_rms(x, ng_ref[2:3, :]), wq_ref[...]) * (MEM_HEAD_DIM ** -0.5)
    kv = kv_ref[0]
    heads = []
    for hh in range(MEM_HEADS):
        lo, hi = hh * MEM_HEAD_DIM, (hh + 1) * MEM_HEAD_DIM
        s = _dot_nt(q[:, lo:hi], kv[:, lo:hi])
        e = jnp.exp(s - jnp.max(s, axis=-1, keepdims=True))
        p = e / jnp.sum(e, axis=-1, keepdims=True)
        heads.append(_dot(p, kv[:, D_MODEL + lo:D_MODEL + hi]))
    o = _dot(jnp.concatenate(heads, axis=1), wmo_ref[...])
    x = x + _rms(o, ng_ref[3:4, :])
    out_ref[...] = x[0:tm].reshape(out_ref.shape)


def _mid(x, ogm, onsa, y, g, bonus, kv_mem, P, *, batch, seq_len, tm):
    n = x.shape[0]
    tiles = seq_len // tm
    if tm >= 8:
        row = lambda w: pl.BlockSpec((tm, w), lambda i: (i, 0))
        shp = lambda t: t
        out_shape = jax.ShapeDtypeStruct((n, D_MODEL), F32)
    else:
        row = lambda w: pl.BlockSpec((1, tm, w), lambda i: (i, 0, 0))
        shp = lambda t: t.reshape(n // tm, tm, t.shape[-1])
        out_shape = jax.ShapeDtypeStruct((n // tm, tm, D_MODEL), F32)
    const = lambda shape: pl.BlockSpec(shape, lambda i: (0,) * len(shape))
    out = pl.pallas_call(
        _mid_kernel,
        grid=(n // tm,),
        in_specs=[row(D_MODEL), row(GM_W), row(NSA_W), row(RW_W), row(RW_W), row(RW_W),
                  const((6, D_MODEL)), const((2, RW_W)), const((D_MODEL, D_MODEL)),
                  const((D_MODEL, D_MODEL)),
                  pl.BlockSpec((1, N_MEM, 2 * D_MODEL), lambda i: (i // tiles, 0, 0)),
                  const((D_MODEL, D_MODEL))],
        out_specs=row(D_MODEL),
        out_shape=out_shape,
        compiler_params=_cparams(("parallel",)),
        name="mix_out_mem_attn",
    )(shp(x), shp(ogm), shp(onsa), shp(y), shp(g), shp(bonus), P['norm_g'], P['rw_vec'][4:6],
      P['w_out'].astype(BF16), P['w_mem_q'].astype(BF16), kv_mem, P['w_mem_o'].astype(BF16))
    return out.reshape(n, D_MODEL)


def _memkv_kernel(x_ref, g_ref, w_ref, o_ref, ob_ref):
    o = _dot(_rms(x_ref[...], g_ref[...]), w_ref[...])
    o_ref[...] = o
    ob_ref[...] = o.astype(BF16)


def _mem_kv(mem, g, w_kv, *, tm):
    n = mem.shape[0]
    return pl.pallas_call(
        _memkv_kernel,
        grid=(n // tm,),
        in_specs=[pl.BlockSpec((tm, D_MODEL), lambda i: (i, 0)), pl.BlockSpec((1, D_MODEL), lambda i: (0, 0)),
                  pl.BlockSpec((D_MODEL, 2 * D_MODEL), lambda i: (0, 0))],
        out_specs=[pl.BlockSpec((tm, 2 * D_MODEL), lambda i: (i, 0))] * 2,
        out_shape=[jax.ShapeDtypeStruct((n, 2 * D_MODEL), F32), jax.ShapeDtypeStruct((n, 2 * D_MODEL), BF16)],
        compiler_params=_cparams(("parallel",)),
        name="mem_kv_proj",
    )(mem, g[None, :], w_kv.astype(BF16))


def _ffn_kernel(x_ref, prev_ref, prev1_ref, ng_ref, wg_ref, wu_ref, cw_ref, cb_ref, wo_ref,
                out_ref, tail_ref, h_scr, acc_scr, tail_scr, *, single, tiles_per_seq):
    tm = x_ref.shape[0]
    i, j = pl.program_id(0), pl.program_id(1)

    @pl.when(j == 0)
    def _():
        h_scr[...] = _rms(x_ref[...], ng_ref[4:5, :]).astype(BF16)
        acc_scr[...] = jnp.zeros_like(acc_scr)

    h = h_scr[...]
    gate = jnp.dot(h, wg_ref[...], preferred_element_type=F32)
    up = jnp.dot(h, wu_ref[...], preferred_element_type=F32)
    if single:
        g2 = prev_ref[...]
        g1 = prev1_ref[...]
        tail_ref[...] = gate
    else:
        first = i % tiles_per_seq == 0
        t0 = jnp.where(first, prev_ref[0, 0:1, :], tail_scr[j, 6:7, :])
        t1 = jnp.where(first, prev_ref[0, 1:2, :], tail_scr[j, 7:8, :])
        rowi = lax.broadcasted_iota(jnp.int32, (tm, 1), 0)
        g1 = jnp.where(rowi == 0, t1, pltpu.roll(gate, 1, 0))
        g2 = jnp.where(rowi == 0, t0, jnp.where(rowi == 1, t1, pltpu.roll(gate, 2, 0)))
        tail_scr[j] = gate[tm - 8:tm]
        tail_ref[0] = gate[tm - 2:tm]
    conv = cb_ref[...] + g2 * cw_ref[0:1, :] + g1 * cw_ref[1:2, :] + gate * cw_ref[2:3, :]
    act = jax.nn.silu(conv) * up
    acc_scr[...] += _dot(act, wo_ref[...])

    @pl.when(j == pl.num_programs(1) - 1)
    def _():
        out_ref[...] = x_ref[...] + _rms(acc_scr[...], ng_ref[5:6, :])


def _ffn(x, conv_prev, P, *, seq_len, tm, tf):
    n = x.shape[0]
    single = seq_len == 1
    nj = D_FF // tf
    w_in = P['ffn_w_in'].astype(BF16)
    if single:
        prev_args = (conv_prev[:, 0], conv_prev[:, 1])
        prev_specs = [pl.BlockSpec((tm, tf), lambda i, j: (i, j))] * 2
        tail_spec = pl.BlockSpec((tm, tf), lambda i, j: (i, j))
        tail_shape = jax.ShapeDtypeStruct((n, D_FF), F32)
        tiles = 1
    else:
        tiles = seq_len // tm
        prev_args = (conv_prev, conv_prev)
        prev_specs = [pl.BlockSpec((1, CONV_W - 1, tf), lambda i, j: (i // tiles, 0, j))] * 2
        tail_spec = pl.BlockSpec((1, CONV_W - 1, tf), lambda i, j: (i, 0, j))
        tail_shape = jax.ShapeDtypeStruct((n // tm, CONV_W - 1, D_FF), F32)
    out, tail = pl.pallas_call(
        functools.partial(_ffn_kernel, single=single, tiles_per_seq=tiles),
        grid=(n // tm, nj),
        in_specs=[pl.BlockSpec((tm, D_MODEL), lambda i, j: (i, 0)), *prev_specs,
                  pl.BlockSpec((6, D_MODEL), lambda i, j: (0, 0)),
                  pl.BlockSpec((D_MODEL, tf), lambda i, j: (0, j)),
                  pl.BlockSpec((D_MODEL, tf), lambda i, j: (0, nj + j)),
                  pl.BlockSpec((CONV_W, tf), lambda i, j: (0, j)),
                  pl.BlockSpec((1, tf), lambda i, j: (0, j)),
                  pl.BlockSpec((tf, D_MODEL), lambda i, j: (j, 0))],
        out_specs=[pl.BlockSpec((tm, D_MODEL), lambda i, j: (i, 0)), tail_spec],
        out_shape=[jax.ShapeDtypeStruct((n, D_MODEL), F32), tail_shape],
        scratch_shapes=[pltpu.VMEM((tm, D_MODEL), BF16), pltpu.VMEM((tm, D_MODEL), F32),
                        pltpu.VMEM((nj, 8, tf), F32)],
        compiler_params=_cparams(("arbitrary", "arbitrary")),
        name="conv_ffn",
    )(x, *prev_args, P['norm_g'], w_in, w_in, P['ffn_conv_w'], P['ffn_conv_b'][None, :],
      P['ffn_w_out'].astype(BF16))
    if single:
        tail = jnp.stack([conv_prev[:, 1], tail], axis=1)
    else:
        tail = tail[tiles - 1::tiles]
    return out, tail


def _sample_cmp_win_kernel(pt_ref, qpad_ref, kvc_hbm, win_ref, kvw_ref, oc_ref, ow_ref, idx_ref,
                           kbuf, sem):
    b = pl.program_id(0)
    n_pages = kbuf.shape[0]
    per_page = PAGE_SIZE // CMP_BLOCK

    def page_copy(p):
        return pltpu.make_async_copy(kvc_hbm.at[pl.ds(pt_ref[b, p], 1)], kbuf.at[pl.ds(p, 1)], sem)

    def start(p, c):
        page_copy(p).start()
        return c

    def wait(p, c):
        page_copy(p).wait()
        return c

    lax.fori_loop(0, n_pages, start, 0)
    lax.fori_loop(0, n_pages, wait, 0)
    q8 = jnp.concatenate([qpad_ref[0, :, h * LANES:(h + 1) * LANES].astype(F32) for h in range(NSA_HEADS)],
                         axis=0)
    ss = [_dot_nt(q8, kbuf[:, j * KV_COLS:j * KV_COLS + LANES]) for j in range(per_page)]
    mx = functools.reduce(jnp.maximum, [jnp.max(s, axis=-1, keepdims=True) for s in ss])
    es = [jnp.exp(s - mx) for s in ss]
    den = functools.reduce(lambda x, y: x + y, [jnp.sum(e, axis=-1, keepdims=True) for e in es])
    ps = [e / den for e in es]
    oc = functools.reduce(lambda x, y: x + y,
                          [_dot(ps[j], kbuf[:, j * KV_COLS + LANES:(j + 1) * KV_COLS]) for j in range(per_page)])
    group0 = lax.broadcasted_iota(jnp.int32, (NSA_HEADS, HEAD_DIM), 0) < NSA_REP
    own_half = lambda o: jnp.where(group0, o[:, 0:HEAD_DIM], o[:, HEAD_DIM:LANES])
    oc_ref[0] = own_half(oc)
    lane = lax.broadcasted_iota(jnp.int32, (NSA_KV_HEADS, n_pages), 1)
    ns_past = 2 * n_pages
    vals = []
    for jj in range(2):
        pj = ps[2 * jj] + ps[2 * jj + 1]
        rows = [jnp.sum(pj[g * NSA_REP:(g + 1) * NSA_REP], axis=0, keepdims=True) for g in range(NSA_KV_HEADS)]
        val = jnp.concatenate(rows, axis=0)
        n_of = 2 * lane + jj
        vals.append(jnp.where(n_of == 0, FORCE, val))
    cur_val = jnp.full((NSA_KV_HEADS, 1), FORCE, F32)
    big = ns_past + 1
    out_lane = lax.broadcasted_iota(jnp.int32, (NSA_KV_HEADS, LANES), 1)
    idx = jnp.zeros((NSA_KV_HEADS, LANES), jnp.int32)
    for rnd in range(TOP_N):
        mx = jnp.maximum(jnp.maximum(jnp.max(vals[0], axis=-1, keepdims=True),
                                     jnp.max(vals[1], axis=-1, keepdims=True)), cur_val)
        c0 = jnp.min(jnp.where(vals[0] == mx, 2 * lane, big), axis=-1, keepdims=True)
        c1 = jnp.min(jnp.where(vals[1] == mx, 2 * lane + 1, big), axis=-1, keepdims=True)
        c2 = jnp.where(cur_val == mx, ns_past, big)
        pick = jnp.minimum(jnp.minimum(c0, c1), c2)
        idx = jnp.where(out_lane == rnd, pick, idx)
        vals[0] = jnp.where(2 * lane == pick, -jnp.inf, vals[0])
        vals[1] = jnp.where(2 * lane + 1 == pick, -jnp.inf, vals[1])
        cur_val = jnp.where(pick == ns_past, -jnp.inf, cur_val)
    idx_ref[0] = jnp.concatenate([idx, jnp.zeros((8 - NSA_KV_HEADS, LANES), jnp.int32)], axis=0)
    wb = win_ref.shape[1]
    s_old = _dot_nt(q8, win_ref[0, :, 0:LANES])
    new_k = jnp.broadcast_to(kvw_ref[0, :, 0:LANES], (8, LANES))
    s_new = _dot_nt(q8, new_k)[:, 0:1]
    keep = lax.broadcasted_iota(jnp.int32, (1, wb), 1) > wb - WINDOW
    s_old = jnp.where(keep, s_old, NEG)
    mx = jnp.maximum(jnp.max(s_old, axis=-1, keepdims=True), s_new)
    e_old = jnp.exp(s_old - mx)
    e_new = jnp.exp(s_new - mx)
    den = jnp.sum(e_old, axis=-1, keepdims=True) + e_new
    new_v = kvw_ref[0, :, LANES:KV_COLS].astype(BF16).astype(F32)
    ow_ref[0] = own_half(_dot(e_old / den, win_ref[0, :, LANES:KV_COLS])
                         + (e_new / den).astype(BF16).astype(F32) * new_v)


def _sample_cmp_win(page_table, qpad, kvc_phys, win_buf, kvw):
    nb, n_pages = page_table.shape
    wb = win_buf.shape[1]
    row_w = (PAGE_SIZE // CMP_BLOCK) * KV_COLS
    piece = jax.ShapeDtypeStruct((nb, NSA_HEADS, HEAD_DIM), F32)
    grid_spec = pltpu.PrefetchScalarGridSpec(
        num_scalar_prefetch=1,
        grid=(nb,),
        in_specs=[pl.BlockSpec((1, 1, NSA_HEADS * LANES), lambda b, pt: (b, 0, 0)),
                  pl.BlockSpec(memory_space=pl.ANY),
                  pl.BlockSpec((1, wb, KV_COLS), lambda b, pt: (b, 0, 0)),
                  pl.BlockSpec((1, 1, KV_COLS), lambda b, pt: (b, 0, 0))],
        out_specs=[pl.BlockSpec((1, NSA_HEADS, HEAD_DIM), lambda b, pt: (b, 0, 0))] * 2
        + [pl.BlockSpec((1, 8, LANES), lambda b, pt: (b, 0, 0))],
        scratch_shapes=[pltpu.VMEM((n_pages, row_w), F32), pltpu.SemaphoreType.DMA(())],
    )
    return pl.pallas_call(
        _sample_cmp_win_kernel,
        grid_spec=grid_spec,
        out_shape=[piece, piece, jax.ShapeDtypeStruct((nb, 8, LANES), jnp.int32)],
        compiler_params=_cparams(("arbitrary",)),
        name="nsa_sample_cmp_win",
    )(page_table, qpad.reshape(nb, 1, NSA_HEADS * LANES), kvc_phys.reshape(-1, row_w), win_buf,
      kvw.reshape(nb, 1, KV_COLS))


def _sample_sel_kernel(pt_ref, idx_ref, q_ref, blk_ref, kvs_ref, gate_ref, oc_ref, ow_ref, out_ref,
                       m_scr, l_scr, acc_scr, *, n_pages):
    b, g, n = pl.program_id(0), pl.program_id(1), pl.program_id(2)

    @pl.when(n == 0)
    def _():
        m_scr[...] = jnp.full_like(m_scr, NEG)
        l_scr[...] = jnp.zeros_like(l_scr)
        acc_scr[...] = jnp.zeros_like(acc_scr)

    q8 = jnp.concatenate([q_ref[0, 0], jnp.zeros((8 - NSA_REP, HEAD_DIM), F32)], axis=0)
    blk = idx_ref[b, g * TOP_N + n]
    is_cur = blk >= 2 * n_pages
    half = blk % (PAGE_SIZE // SLC_BLOCK)
    kv_new = kvs_ref[0]
    k_new = jnp.where(g == 0, kv_new[0:1], kv_new[1:2])
    v_new = jnp.where(g == 0, kv_new[2:3], kv_new[3:4])
    k_t = blk_ref[0, 0, 0, 0]
    v_t = blk_ref[0, 0, 1, 0]
    s_page = _dot(q8, k_t)
    s_new = _dot_nt(q8, jnp.broadcast_to(k_new, (8, HEAD_DIM)))[:, 0:1]
    lane = lax.broadcasted_iota(jnp.int32, (8, PAGE_SIZE), 1)
    s = jnp.where(is_cur, jnp.where(lane == 0, s_new, NEG), jnp.where(lane // SLC_BLOCK == half, s_page, NEG))
    m_old = m_scr[...]
    m_new = jnp.maximum(m_old, jnp.max(s, axis=-1, keepdims=True))
    alpha = jnp.exp(m_old - m_new)
    p = jnp.exp(s - m_new[:, 0:1])
    pv_new = p[:, 0:1].astype(BF16).astype(F32) * v_new.astype(BF16).astype(F32)
    pv = jnp.where(is_cur, pv_new, _dot_nt(p, v_t))
    l_scr[...] = alpha * l_scr[...] + jnp.sum(p, axis=-1, keepdims=True)
    acc_scr[...] = alpha[:, 0:HEAD_DIM] * acc_scr[...] + pv
    m_scr[...] = m_new

    @pl.when(n == pl.num_programs(2) - 1)
    def _():
        o_s = (acc_scr[...] / l_scr[:, 0:HEAD_DIM])[0:NSA_REP]
        head = g * NSA_REP + lax.broadcasted_iota(jnp.int32, (NSA_REP, GATE_PAD), 0)
        lane_g = lax.broadcasted_iota(jnp.int32, (NSA_REP, GATE_PAD), 1)
        gate = jnp.broadcast_to(gate_ref[0], (NSA_REP, GATE_PAD))
        gcol = lambda br: jnp.sum(jnp.where(lane_g == br * NSA_HEADS + head, gate, 0.0), axis=-1, keepdims=True)
        out_ref[0, 0] = gcol(0) * oc_ref[0, 0] + gcol(1) * o_s + gcol(2) * ow_ref[0, 0]


def _sample_sel(page_table, idx, q, pool_pages, layer, kvs, gates, oc, ow):
    nb, n_pages = page_table.shape
    halves = PAGE_SIZE // SLC_BLOCK

    def blk_map(b, g, n, pt, ix):
        page = jnp.minimum(ix[b, g * TOP_N + n] // halves, n_pages - 1)
        return (layer, pt[b, page], 0, g, 0, 0)

    per_bg = pl.BlockSpec((1, 1, NSA_REP, HEAD_DIM), lambda b, g, n, pt, ix: (b, g, 0, 0))
    per_b = lambda shape: pl.BlockSpec((1,) + shape, lambda b, g, n, pt, ix: (b, 0, 0))
    grid_spec = pltpu.PrefetchScalarGridSpec(
        num_scalar_prefetch=2,
        grid=(nb, NSA_KV_HEADS, TOP_N),
        in_specs=[per_bg, pl.BlockSpec((1, 1, 2, 1, HEAD_DIM, PAGE_SIZE), blk_map),
                  per_b((2 * NSA_KV_HEADS, HEAD_DIM)), per_b((1, GATE_PAD)), per_bg, per_bg],
        out_specs=per_bg,
        scratch_shapes=[pltpu.VMEM((8, LANES), F32), pltpu.VMEM((8, LANES), F32),
                        pltpu.VMEM((8, HEAD_DIM), F32)],
    )
    grouped = lambda t: t.reshape(nb, NSA_KV_HEADS, NSA_REP, HEAD_DIM)
    out = pl.pallas_call(
        functools.partial(_sample_sel_kernel, n_pages=n_pages),
        grid_spec=grid_spec,
        out_shape=jax.ShapeDtypeStruct((nb, NSA_KV_HEADS, NSA_REP, HEAD_DIM), F32),
        compiler_params=_cparams(("arbitrary", "arbitrary", "arbitrary")),
        name="nsa_sample_sel",
    )(page_table, idx, grouped(q), pool_pages, kvs.reshape(nb, 2 * NSA_KV_HEADS, HEAD_DIM),
      gates.reshape(nb, 1, GATE_PAD), grouped(oc), grouped(ow))
    return out.reshape(nb, NSA_W)


def _tile(n, want):
    t = min(n, want)
    while n % t or (t % 8 and t != n):
        t -= 1
    return t


def _prompt_layer(x, tabs, mem_prompt, P, *, batch, seq_len):
    kv_f32, kv_b16 = _mem_kv(mem_prompt, P['mem_g'], P['w_mem_kv'], tm=_tile(mem_prompt.shape[0], 256))
    w_pack = _pack_w_in(P['w_in'])
    ogm, _, qpad, kvc, kvs, kvw, kvb, gates, zr = _in_proj(
        x, P['norm_g'][0], w_pack, tabs[0], tabs[1], P['gm_ln'], P['gm_ws'], P['gm_bs'],
        seq_len=seq_len, tm=_tile(seq_len, 256))
    pe, wb = _compress_weights(P['cmp_pe'], P['cmp_w'])
    nc = seq_len // CMP_BLOCK
    comp = _compress(kvc, pe, wb, tr=seq_len)
    comp = comp.reshape(batch, nc // 2, 2, KV_COLS).transpose(0, 2, 1, 3).reshape(batch, nc, KV_COLS)
    oc, sel = _cmp_attn(qpad, comp, batch=batch, seq_len=seq_len, tq=_tile(seq_len, 256))
    onsa = _sel_win(qpad, sel, gates, oc, kvb, batch=batch, seq_len=seq_len, tq=128)
    shift0 = jnp.zeros((batch, RW_COLS), F32)
    r, w, k, v, a, b, g, bonus = _rw_prep(zr, shift0, P, seq_len=seq_len, tm=_tile(seq_len, 512))
    s0 = jnp.zeros((batch, RW_HEADS, HEAD_DIM, HEAD_DIM), F32)
    y, s_fin = _rw_scan(r, w, k, v, a, b, s0, batch=batch, seq_len=seq_len, tc=_tile(seq_len, 256))
    x = _mid(x, ogm, onsa, y, g, bonus, kv_b16.reshape(batch, N_MEM, 2 * D_MODEL), P,
             batch=batch, seq_len=seq_len, tm=_tile(seq_len, 256))
    conv0 = jnp.zeros((batch, CONV_W - 1, D_FF), F32)
    x, conv_new = _ffn(x, conv0, P, seq_len=seq_len, tm=_tile(seq_len, 512), tf=256)
    kvshape = (batch, seq_len, 2, NSA_KV_HEADS, HEAD_DIM)
    wbp = min(WINDOW, seq_len)
    states = dict(
        cmp=kvc.reshape(kvshape), slc=kvs.reshape(kvshape), win=kvw.reshape(kvshape)[:, seq_len - wbp:],
        rw=s_fin, shift=zr.reshape(batch, seq_len, RW_COLS)[:, -1], conv=conv_new,
        mem=kv_f32.reshape(batch, N_MEM, 2, MEM_HEADS, MEM_HEAD_DIM))
    return x, states


def _sample_layer(x, tabs, page_table, cmp_pages, slc_pages, layer, win_buf, mem_kv, rw_state, shift_prev,
                  conv_prev, P):
    nb = x.shape[0]
    w_pack = _pack_w_in(P['w_in'])
    ogm, vgm, qpad, kvc, kvs, kvw, _, gates, zr, q = _in_proj(
        x, P['norm_g'][0], w_pack, tabs[0], tabs[1], P['gm_ln'], P['gm_ws'], P['gm_bs'], seq_len=1, tm=nb)
    pe, wb = _compress_weights(P['cmp_pe'], P['cmp_w'])
    n_phys = cmp_pages.shape[1]
    kvc_phys = _compress_pages(cmp_pages, layer, pe, wb, pages=_tile(n_phys, 64))
    oc, ow, idx = _sample_cmp_win(page_table, qpad, kvc_phys, win_buf.reshape(nb, -1, KV_COLS), kvw)
    idx = idx[:, :NSA_KV_HEADS, :TOP_N].reshape(nb, NSA_KV_HEADS * TOP_N)
    slc6 = slc_pages.reshape(slc_pages.shape[0], n_phys, 2, NSA_KV_HEADS, HEAD_DIM, PAGE_SIZE)
    onsa = _sample_sel(page_table, idx, q, slc6, layer, kvs, gates, oc, ow)
    r, w, k, v, a, b, g, bonus = _rw_prep(zr, shift_prev, P, seq_len=1, tm=nb)
    y, s_fin = _rw_scan(r, w, k, v, a, b, rw_state, batch=nb, seq_len=1, tc=1)
    x = _mid(x, ogm, onsa, y, g, bonus, mem_kv.reshape(nb, N_MEM, 2 * D_MODEL), P, batch=nb, seq_len=1, tm=1)
    x, conv_new = _ffn(x, conv_prev, P, seq_len=1, tm=nb, tf=256)
    kvshape = (nb, 1, 2, NSA_KV_HEADS, HEAD_DIM)
    states = dict(cmp=kvc.reshape(kvshape), slc=kvs.reshape(kvshape), win=kvw.reshape(kvshape), rw=s_fin,
                  shift=zr, conv=conv_new, gv=vgm.reshape(nb, 1, GM_W))
    return x, states


def kernel(x_prompt, x_sample, cache_cmp_kv, cache_slc_kv, cache_win_kv, cache_mem_kv, state_rwkv, state_rwkv_shift, state_ffn_conv, page_table, mem_prompt, norm_g, w_in, w_out, gm_ln, gm_ws, gm_bs, cmp_pe, cmp_w, rw_mu, rw_vec, rw_w2, rw_a2, rw_g2, rw_rk, mem_g, w_mem_q, w_mem_kv, w_mem_o, ffn_w_in, ffn_conv_w, ffn_conv_b, ffn_w_out):
    bp, tp = x_prompt.shape[:2]
    nb, ts = x_sample.shape[:2]
    assert ts == 1 and tp % 512 == 0
    depth = norm_g.shape[0]
    past = page_table.shape[1] * PAGE_SIZE
    tabs_p = _rope_tables(jnp.arange(tp))
    tabs_s = _rope_tables(jnp.full((nb,), past))
    xp = x_prompt.reshape(bp * tp, D_MODEL)
    xs = x_sample.reshape(nb, D_MODEL)
    mem = mem_prompt.reshape(bp * N_MEM, D_MODEL)
    weights = dict(norm_g=norm_g, w_in=w_in, w_out=w_out, gm_ln=gm_ln, gm_ws=gm_ws, gm_bs=gm_bs,
                   cmp_pe=cmp_pe, cmp_w=cmp_w, rw_mu=rw_mu, rw_vec=rw_vec, rw_w2=rw_w2, rw_a2=rw_a2,
                   rw_g2=rw_g2, rw_rk=rw_rk, mem_g=mem_g, w_mem_q=w_mem_q, w_mem_kv=w_mem_kv,
                   w_mem_o=w_mem_o, ffn_w_in=ffn_w_in, ffn_conv_w=ffn_conv_w, ffn_conv_b=ffn_conv_b,
                   ffn_w_out=ffn_w_out)
    cmp_pages = _pool_pages(cache_cmp_kv)
    slc_pages = _pool_pages(cache_slc_kv)
    ps, ss = [], []
    for l in range(depth):
        P = {name: val[l] for name, val in weights.items()}
        xp, st = _prompt_layer(xp, tabs_p, mem, P, batch=bp, seq_len=tp)
        ps.append(st)
        xs, st = _sample_layer(xs, tabs_s, page_table, cmp_pages, slc_pages, l, cache_win_kv[l],
                               cache_mem_kv[l], state_rwkv[l], state_rwkv_shift[l], state_ffn_conv[l], P)
        ss.append(st)
    stack = lambda sts, name: jnp.stack([st[name] for st in sts])
    return (xp.reshape(bp, tp, D_MODEL), xs.reshape(nb, 1, D_MODEL),
            stack(ps, 'cmp'), stack(ps, 'slc'), stack(ps, 'win'), stack(ps, 'rw'), stack(ps, 'shift'),
            stack(ps, 'conv'), stack(ps, 'mem'),
            stack(ss, 'cmp'), stack(ss, 'slc'), stack(ss, 'win'), stack(ss, 'rw'), stack(ss, 'shift'),
            stack(ss, 'conv'), stack(ss, 'gv'))
```

```python
import functools
import math

import jax
import jax.numpy as jnp
from jax import lax
from jax.experimental import pallas as pl
from jax.experimental.pallas import tpu as pltpu

F32 = jnp.float32
BF16 = jnp.bfloat16

D_MODEL = 1024
HEAD_DIM = 64
GM_W = 256
GM_HEADS = 4
CHUNK = 128
NSA_W = 512
NSA_HEADS = 8
NSA_KV_HEADS = 2
NSA_REP = 4
CMP_BLOCK = 32
SLC_BLOCK = 64
TOP_N = 16
WINDOW = 512
RW_W = 256
RW_HEADS = 4
W_LORA = 64
A_LORA = 64
G_LORA = 128
RW_COLS = 3 * RW_W + W_LORA + A_LORA + G_LORA
RW_LN_EPS = 64e-5
N_MEM = 256
MEM_HEADS = 4
MEM_HEAD_DIM = 256
D_FF = 2816
CONV_W = 3
PAGE_SIZE = 128
ROPE_THETA = 10000.0
NORM_EPS = 1e-6
LN_EPS = 1e-5
NEG = -1e30
FORCE = 1e4
OFF_GM = 0
OFF_Q = 512
OFF_KV = 1024
OFF_GATE = 1792
OFF_RW = 1816
IN_COLS = 2840
KV_COLS = 2 * NSA_KV_HEADS * HEAD_DIM
CMP_K = CMP_BLOCK * KV_COLS
GATE_PAD = 128
PACK_COLS = OFF_GATE + RW_COLS + GATE_PAD

LANES = 128
VMEM_LIMIT = 56 * 1024 * 1024


def _cparams(sem):
    return pltpu.CompilerParams(dimension_semantics=sem, vmem_limit_bytes=VMEM_LIMIT)


def _rms(x, g):
    return x * lax.rsqrt(jnp.mean(x * x, axis=-1, keepdims=True) + NORM_EPS) * g


def _dot(a, b):
    return jnp.dot(a.astype(BF16), b.astype(BF16), preferred_element_type=F32)


def _dot_nt(a, b):
    return lax.dot_general(a.astype(BF16), b.astype(BF16), (((1,), (1,)), ((), ())),
                           preferred_element_type=F32)


def _lane_lo(shape):
    return lax.broadcasted_iota(jnp.int32, shape, len(shape) - 1) % LANES < HEAD_DIM


def _head_sum(x):
    cols = []
    for c in range(x.shape[1] // LANES):
        xc = x[:, c * LANES:(c + 1) * LANES]
        lo_mask = _lane_lo(xc.shape)
        lo = jnp.sum(jnp.where(lo_mask, xc, 0.0), axis=-1, keepdims=True)
        hi = jnp.sum(jnp.where(lo_mask, 0.0, xc), axis=-1, keepdims=True)
        cols.append(jnp.where(lo_mask, lo, hi))
    return jnp.concatenate(cols, axis=1)


def _inproj_kernel(x_ref, g_ref, w_ref, cos_ref, sin_ref, ln_ref, ws_ref, bias_ref,
                   ogm_ref, vgm_ref, qpad_ref, kvc_ref, kvs_ref, kvw_ref, kvb_ref, gate_ref, zr_ref, *q_ref,
                   single):
    tm = x_ref.shape[0]
    h = _rms(x_ref[...], g_ref[...])
    z = _dot(h, w_ref[...])
    zg = jax.nn.gelu(z[:, 0:2 * GM_W])
    u = zg[:, :GM_W]
    vv = zg[:, GM_W:]
    mean = jnp.mean(vv, axis=-1, keepdims=True)
    var = jnp.mean(jnp.square(vv - mean), axis=-1, keepdims=True)
    v = (vv - mean) * lax.rsqrt(var + LN_EPS) * ln_ref[0:1, :] + ln_ref[1:2, :]
    vgm_ref[...] = v
    if single:
        ogm_ref[...] = (u * (v * ws_ref[...] + bias_ref[...])).astype(BF16)
    else:
        head_of_lane = lax.broadcasted_iota(jnp.int32, (CHUNK, GM_W), 1) // HEAD_DIM
        causal = (lax.broadcasted_iota(jnp.int32, (CHUNK, CHUNK), 0)
                  >= lax.broadcasted_iota(jnp.int32, (CHUNK, CHUNK), 1))
        wms = [jnp.where(causal, ws_ref[hh], 0.0).astype(BF16) for hh in range(GM_HEADS)]
        for c in range(tm // CHUNK):
            vc = v[c * CHUNK:(c + 1) * CHUNK]
            acc = bias_ref[...]
            for hh in range(GM_HEADS):
                vm = jnp.where(head_of_lane == hh, vc, 0.0).astype(BF16)
                acc = acc + jnp.dot(wms[hh], vm, preferred_element_type=F32)
            ogm_ref[c * CHUNK:(c + 1) * CHUNK, :] = (u[c * CHUNK:(c + 1) * CHUNK] * acc).astype(BF16)
    cos = cos_ref[...]
    sin = sin_ref[...]
    first_half = lax.broadcasted_iota(jnp.int32, (tm, LANES), 1) % HEAD_DIM < HEAD_DIM // 2
    lane_lo = _lane_lo((tm, LANES))

    def rope(xc):
        rot = jnp.where(first_half, pltpu.roll(xc, LANES - HEAD_DIM // 2, 1), pltpu.roll(xc, HEAD_DIM // 2, 1))
        return xc * cos + rot * sin

    scale = HEAD_DIM ** -0.5
    for c in range(NSA_W // LANES):
        qc = rope(z[:, OFF_Q + c * LANES:OFF_Q + (c + 1) * LANES]) * scale
        if single:
            q_ref[0][:, c * LANES:(c + 1) * LANES] = qc
        qr = pltpu.roll(qc, HEAD_DIM, 1)
        if (2 * c) // NSA_REP == 0:
            even = jnp.where(lane_lo, qc, 0.0)
            odd = jnp.where(lane_lo, qr, 0.0)
        else:
            even = jnp.where(lane_lo, 0.0, qr)
            odd = jnp.where(lane_lo, 0.0, qc)
        qpad_ref[:, (2 * c) * LANES:(2 * c + 1) * LANES] = even.astype(BF16)
        qpad_ref[:, (2 * c + 1) * LANES:(2 * c + 2) * LANES] = odd.astype(BF16)
    for j, ref in enumerate((kvc_ref, kvs_ref, kvw_ref)):
        base = OFF_KV + j * KV_COLS
        kk = rope(z[:, base:base + LANES])
        vj = z[:, base + LANES:base + KV_COLS]
        ref[:, 0:LANES] = kk
        ref[:, LANES:KV_COLS] = vj
        if j >= 1:
            kvb_ref[:, (j - 1) * KV_COLS:(j - 1) * KV_COLS + LANES] = kk.astype(BF16)
            kvb_ref[:, (j - 1) * KV_COLS + LANES:j * KV_COLS] = vj.astype(BF16)
    zr_ref[...] = z[:, OFF_GATE:OFF_GATE + RW_COLS]
    gate_ref[...] = jax.nn.sigmoid(z[:, OFF_GATE + RW_COLS:])


def _pack_w_in(w_in):
    gate = w_in[:, OFF_GATE:OFF_RW].reshape(D_MODEL, NSA_HEADS, 3)
    gate = jnp.transpose(gate, (0, 2, 1)).reshape(D_MODEL, 3 * NSA_HEADS)
    gate = jnp.pad(gate, ((0, 0), (0, GATE_PAD - 3 * NSA_HEADS)))
    return jnp.concatenate([w_in[:, :OFF_GATE], w_in[:, OFF_RW:], gate], axis=1).astype(BF16)


def _in_proj(x, g0, w_pack, cos, sin, gm_ln, gm_ws, gm_bs, *, seq_len, tm):
    n = x.shape[0]
    single = seq_len == 1
    if single:
        ws_arg = jnp.repeat(gm_ws[:, 0, 0], HEAD_DIM)[None, :]
        bias_arg = jnp.repeat(gm_bs[:, 0], HEAD_DIM)[None, :]
        ws_spec = pl.BlockSpec((1, GM_W), lambda i: (0, 0))
        bias_spec = pl.BlockSpec((1, GM_W), lambda i: (0, 0))
        tab_map = lambda i: (i, 0)
    else:
        assert seq_len % tm == 0 and tm % CHUNK == 0
        ws_arg = gm_ws
        bias_arg = jnp.repeat(gm_bs.T, HEAD_DIM, axis=1)
        ws_spec = pl.BlockSpec((GM_HEADS, CHUNK, CHUNK), lambda i: (0, 0, 0))
        bias_spec = pl.BlockSpec((CHUNK, GM_W), lambda i: (0, 0))
        tiles = seq_len // tm
        tab_map = lambda i: (i % tiles, 0)
    row = lambda w: pl.BlockSpec((tm, w), lambda i: (i, 0))
    outs = [(GM_W, BF16), (GM_W, F32), (NSA_HEADS * LANES, BF16), (KV_COLS, F32), (KV_COLS, F32),
            (KV_COLS, F32), (2 * KV_COLS, BF16), (GATE_PAD, F32), (RW_COLS, F32)]
    if single:
        outs.append((NSA_W, F32))
    return pl.pallas_call(
        functools.partial(_inproj_kernel, single=single),
        grid=(n // tm,),
        in_specs=[row(D_MODEL), pl.BlockSpec((1, D_MODEL), lambda i: (0, 0)),
                  pl.BlockSpec((D_MODEL, PACK_COLS), lambda i: (0, 0)),
                  pl.BlockSpec((tm, LANES), tab_map), pl.BlockSpec((tm, LANES), tab_map),
                  pl.BlockSpec((2, GM_W), lambda i: (0, 0)), ws_spec, bias_spec],
        out_specs=[row(w) for w, _ in outs],
        out_shape=[jax.ShapeDtypeStruct((n, w), dt) for w, dt in outs],
        compiler_params=_cparams(("parallel",)),
        name="in_proj",
    )(x, g0[None, :], w_pack, cos, sin, gm_ln, ws_arg, bias_arg)


def _rope_tables(pos):
    half = HEAD_DIM // 2
    inv = ROPE_THETA ** (-jnp.arange(half, dtype=F32) / half)
    ang = pos.astype(F32)[:, None] * inv[None, :]
    cos, sin = jnp.cos(ang), jnp.sin(ang)
    return jnp.tile(cos, (1, 4)), jnp.tile(jnp.concatenate([-sin, sin], axis=1), (1, 2))


def _compress_rows(k_rows_ref, v_rows_ref, pe_ref, w_ref, n_out):
    acc_k = jnp.zeros((n_out, LANES), F32)
    acc_v = jnp.zeros((n_out, LANES), F32)
    for l in range(CMP_BLOCK):
        xk = k_rows_ref[pl.ds(l, n_out, stride=CMP_BLOCK), :] + pe_ref[l:l + 1, 0:LANES]
        xv = v_rows_ref[pl.ds(l, n_out, stride=CMP_BLOCK), :] + pe_ref[l:l + 1, LANES:KV_COLS]
        acc_k = acc_k + _dot(xk, w_ref[l, 0:LANES, 0:LANES])
        acc_v = acc_v + _dot(xv, w_ref[l, LANES:KV_COLS, LANES:KV_COLS])
    return jnp.concatenate([acc_k, acc_v], axis=1)


def _compress_kernel(k_ref, v_ref, pe_ref, w_ref, o_ref):
    o_ref[...] = _compress_rows(k_ref, v_ref, pe_ref, w_ref, o_ref.shape[0]).astype(o_ref.dtype)


def _compress_pages_kernel(x_ref, pe_ref, w_ref, o_ref, k_scr, v_scr):
    def transpose_page(p, c):
        rows = pl.ds(pl.multiple_of(p * PAGE_SIZE, PAGE_SIZE), PAGE_SIZE)
        k_scr[rows, :] = x_ref[0, p, 0:LANES, :].T
        v_scr[rows, :] = x_ref[0, p, LANES:KV_COLS, :].T
        return c

    lax.fori_loop(0, x_ref.shape[1], transpose_page, 0)
    o_ref[...] = _compress_rows(k_scr, v_scr, pe_ref, w_ref, o_ref.shape[0])


def _compress_weights(cmp_pe, cmp_w):
    eye = jnp.eye(2 * NSA_KV_HEADS, dtype=F32).reshape(2, NSA_KV_HEADS, 2, NSA_KV_HEADS)
    wb = jnp.einsum('lsde,sgtq->lsgdtqe', cmp_w, eye).reshape(CMP_BLOCK, KV_COLS, KV_COLS).astype(BF16)
    pe = jnp.broadcast_to(cmp_pe[:, :, None, :], (CMP_BLOCK, 2, NSA_KV_HEADS, HEAD_DIM)).reshape(CMP_BLOCK, KV_COLS)
    return pe, wb


def _cmp_const_specs():
    return [pl.BlockSpec((CMP_BLOCK, KV_COLS), lambda i: (0, 0)),
            pl.BlockSpec((CMP_BLOCK, KV_COLS, KV_COLS), lambda i: (0, 0, 0))]


def _compress(x, pe, wb, *, tr):
    n = x.shape[0]
    return pl.pallas_call(
        _compress_kernel,
        grid=(n // tr,),
        in_specs=[pl.BlockSpec((tr, LANES), lambda i: (i, 0)), pl.BlockSpec((tr, LANES), lambda i: (i, 1))]
        + _cmp_const_specs(),
        out_specs=pl.BlockSpec((tr // CMP_BLOCK, KV_COLS), lambda i: (i, 0)),
        out_shape=jax.ShapeDtypeStruct((n // CMP_BLOCK, KV_COLS), BF16),
        compiler_params=_cparams(("parallel",)),
        name="nsa_compress",
    )(x, x, pe, wb)


def _pool_pages(pool):
    depth, n_phys = pool.shape[:2]
    return jnp.transpose(pool, (0, 1, 3, 4, 5, 2)).reshape(depth, n_phys, KV_COLS, PAGE_SIZE)


def _compress_pages(pool_pages, layer, pe, wb, *, pages):
    n_phys = pool_pages.shape[1]
    per_page = PAGE_SIZE // CMP_BLOCK
    return pl.pallas_call(
        _compress_pages_kernel,
        grid=(n_phys // pages,),
        in_specs=[pl.BlockSpec((1, pages, KV_COLS, PAGE_SIZE), lambda i: (layer, i, 0, 0))] + _cmp_const_specs(),
        out_specs=pl.BlockSpec((pages * per_page, KV_COLS), lambda i: (i, 0)),
        out_shape=jax.ShapeDtypeStruct((n_phys * per_page, KV_COLS), F32),
        scratch_shapes=[pltpu.VMEM((pages * PAGE_SIZE, LANES), F32)] * 2,
        compiler_params=_cparams(("parallel",)),
        name="nsa_compress_pages",
    )(pool_pages, pe, wb)


def _group_q(qpad_ref, g):
    return jnp.concatenate([qpad_ref[:, (g * NSA_REP + r) * LANES:(g * NSA_REP + r + 1) * LANES]
                            for r in range(NSA_REP)], axis=0)


def _assemble_heads(pieces, tq):
    lane_lo = _lane_lo((tq, LANES))
    cols = []
    for c in range(NSA_W // LANES):
        g = (2 * c) // NSA_REP
        r = (2 * c) % NSA_REP
        even = pieces[g][r * tq:(r + 1) * tq]
        odd = pieces[g][(r + 1) * tq:(r + 2) * tq]
        if g == 0:
            cols.append(jnp.where(lane_lo, even, pltpu.roll(odd, HEAD_DIM, 1)))
        else:
            cols.append(jnp.where(lane_lo, pltpu.roll(even, HEAD_DIM, 1), odd))
    return cols


def _cmp_attn_kernel(qpad_ref, kvc_ref, oc_ref, sel_ref):
    tq = qpad_ref.shape[0]
    q0 = pl.program_id(1) * tq
    kvc = kvc_ref[0]
    nc = kvc.shape[0]
    ns = nc // 2
    coli = lax.broadcasted_iota(jnp.int32, (1, nc), 1)
    cblk = jnp.where(coli < ns, 2 * coli, 2 * (coli - ns) + 1)
    c_end = cblk * CMP_BLOCK + (CMP_BLOCK - 1)
    rowpos = q0 + lax.broadcasted_iota(jnp.int32, (NSA_REP * tq, 1), 0) % tq
    m_c = c_end <= rowpos
    blk = lax.broadcasted_iota(jnp.int32, (ns, tq), 0)
    cur = (q0 + lax.broadcasted_iota(jnp.int32, (ns, tq), 1)) // SLC_BLOCK
    pieces = []
    for g in range(NSA_KV_HEADS):
        qg = _group_q(qpad_ref, g)
        s = _dot_nt(qg, kvc[:, 0:LANES])
        sm = jnp.where(m_c, s, NEG)
        e = jnp.exp(sm - jnp.max(sm, axis=-1, keepdims=True))
        p = jnp.where(m_c, e / jnp.sum(e, axis=-1, keepdims=True), 0.0)
        pieces.append(_dot(p, kvc[:, LANES:KV_COLS]))
        ps = p[0:tq] + p[tq:2 * tq] + p[2 * tq:3 * tq] + p[3 * tq:4 * tq]
        imp = (ps[:, :ns] + ps[:, ns:]).T
        imp = jnp.where((blk == cur) | (blk == 0), FORCE, imp)
        imp = jnp.where(blk > cur, NEG, imp)
        work = imp
        sel = jnp.zeros((ns, tq), F32)
        for _ in range(min(TOP_N, ns)):
            mx = jnp.max(work, axis=0, keepdims=True)
            first = jnp.min(jnp.where(work == mx, blk, ns), axis=0, keepdims=True)
            pick = blk == first
            sel = jnp.where(pick, 1.0, sel)
            work = jnp.where(pick, -jnp.inf, work)
        sel = jnp.where(imp > 0.5 * NEG, sel, 0.0)
        sel_ref[:, g * ns:(g + 1) * ns] = sel.T.astype(BF16)
    cols = _assemble_heads(pieces, tq)
    for c in range(NSA_W // LANES):
        oc_ref[:, c * LANES:(c + 1) * LANES] = cols[c]


def _cmp_attn(qpad, kvc_perm, *, batch, seq_len, tq):
    n = qpad.shape[0]
    nq = seq_len // tq
    nc = kvc_perm.shape[1]
    return pl.pallas_call(
        _cmp_attn_kernel,
        grid=(batch, nq),
        in_specs=[pl.BlockSpec((tq, NSA_HEADS * LANES), lambda b, i: (b * nq + i, 0)),
                  pl.BlockSpec((1, nc, KV_COLS), lambda b, i: (b, 0, 0))],
        out_specs=[pl.BlockSpec((tq, NSA_W), lambda b, i: (b * nq + i, 0)),
                   pl.BlockSpec((tq, nc), lambda b, i: (b * nq + i, 0))],
        out_shape=[jax.ShapeDtypeStruct((n, NSA_W), F32), jax.ShapeDtypeStruct((n, nc), BF16)],
        compiler_params=_cparams(("parallel", "parallel")),
        name="nsa_cmp_attn",
    )(qpad, kvc_perm)


SEL_KV_TILE = 512


def _gate_col(gate, br, c, tq):
    lane_lo = _lane_lo((tq, LANES))
    i0 = br * NSA_HEADS + 2 * c
    return jnp.where(lane_lo, gate[:, i0:i0 + 1], gate[:, i0 + 1:i0 + 2])


def _sel_win_kernel(qpad_ref, sel_ref, gate_ref, oc_ref, kvb_ref, out_ref, *, seq_len):
    tq = qpad_ref.shape[0]
    rows = NSA_REP * tq
    kt = min(SEL_KV_TILE, seq_len)
    ns = seq_len // SLC_BLOCK
    q0 = pl.program_id(1) * tq
    assert kt % tq == 0
    j_diag = q0 // kt
    key_blk = lax.broadcasted_iota(jnp.int32, (kt, ns), 0) // SLC_BLOCK
    blk_lane = lax.broadcasted_iota(jnp.int32, (kt, ns), 1)
    lane_kt = lax.broadcasted_iota(jnp.int32, (1, kt), 1)
    n_win = (WINDOW + tq) // LANES
    all_rows = NSA_KV_HEADS * rows
    rowpos = q0 + lax.broadcasted_iota(jnp.int32, (all_rows, 1), 0) % tq
    qs, unsels = [], []
    for g in range(NSA_KV_HEADS):
        unsel = (1.0 - sel_ref[:, g * ns:(g + 1) * ns].astype(F32)).astype(BF16)
        qs.append(_group_q(qpad_ref, g))
        unsels.extend([unsel] * NSA_REP)
    q_all = jnp.concatenate(qs, axis=0)
    lhs = jnp.concatenate([q_all, jnp.concatenate(unsels, axis=0)], axis=1)
    ones = jnp.ones((1, LANES), BF16)

    def with_ones(v):
        return jnp.concatenate([v, jnp.broadcast_to(ones, v.shape)], axis=1)

    def tile(j, carry, diagonal):
        m, acc = carry
        start = pl.multiple_of(j * kt, kt)
        k = kvb_ref[0, pl.ds(start, kt), 0:LANES]
        v = kvb_ref[0, pl.ds(start, kt), LANES:KV_COLS]
        penalty = jnp.where(blk_lane == key_blk + j * (kt // SLC_BLOCK), NEG, 0.0).astype(BF16)
        s = _dot_nt(lhs, jnp.concatenate([k, penalty], axis=1))
        if diagonal:
            s = jnp.where(start + lane_kt <= rowpos, s, NEG)
        m_new = jnp.maximum(m, jnp.max(s, axis=-1, keepdims=True))
        p = jnp.exp(s - m_new).astype(BF16)
        acc = jnp.exp(m - m_new) * acc + jnp.dot(p, with_ones(v), preferred_element_type=F32)
        return m_new, acc

    init = (jnp.full((all_rows, 1), NEG, F32), jnp.zeros((all_rows, 2 * LANES), F32))
    carry = lax.fori_loop(0, j_diag, functools.partial(tile, diagonal=False), init)
    _, acc = tile(j_diag, carry, True)
    o_sel = acc[:, 0:LANES] / acc[:, LANES:2 * LANES]
    ss, vs = [], []
    for j in range(n_win):
        start_true = q0 - WINDOW + j * LANES
        start = pl.multiple_of(jnp.maximum(start_true, 0), LANES)
        ss.append(_dot_nt(q_all, kvb_ref[0, pl.ds(start, LANES), KV_COLS:KV_COLS + LANES]))
        vs.append(kvb_ref[0, pl.ds(start, LANES), KV_COLS + LANES:2 * KV_COLS])
    kpos = q0 - WINDOW + lax.broadcasted_iota(jnp.int32, (1, n_win * LANES), 1)
    valid = (kpos <= rowpos) & (kpos > rowpos - WINDOW) & (kpos >= 0)
    s = jnp.where(valid, jnp.concatenate(ss, axis=1), NEG)
    e = jnp.where(valid, jnp.exp(s - jnp.max(s, axis=-1, keepdims=True)), 0.0).astype(BF16)
    acc = jnp.dot(e, with_ones(jnp.concatenate(vs, axis=0)), preferred_element_type=F32)
    o_win = acc[:, 0:LANES] / acc[:, LANES:2 * LANES]
    sel_pieces = [o_sel[g * rows:(g + 1) * rows] for g in range(NSA_KV_HEADS)]
    win_pieces = [o_win[g * rows:(g + 1) * rows] for g in range(NSA_KV_HEADS)]
    sel_cols = _assemble_heads(sel_pieces, tq)
    win_cols = _assemble_heads(win_pieces, tq)
    gate = gate_ref[...]
    for c in range(NSA_W // LANES):
        oc = oc_ref[:, c * LANES:(c + 1) * LANES]
        o = (_gate_col(gate, 0, c, tq) * oc + _gate_col(gate, 1, c, tq) * sel_cols[c]
             + _gate_col(gate, 2, c, tq) * win_cols[c])
        out_ref[:, c * LANES:(c + 1) * LANES] = o.astype(BF16)


def _sel_win(qpad, sel, gates, oc, kvb, *, batch, seq_len, tq):
    n = qpad.shape[0]
    nq = seq_len // tq
    ns2 = sel.shape[1]
    row = lambda w: pl.BlockSpec((tq, w), lambda b, i: (b * nq + i, 0))
    return pl.pallas_call(
        functools.partial(_sel_win_kernel, seq_len=seq_len),
        grid=(batch, nq),
        in_specs=[row(NSA_HEADS * LANES), row(ns2), row(GATE_PAD), row(NSA_W),
                  pl.BlockSpec((1, seq_len, 2 * KV_COLS), lambda b, i: (b, 0, 0))],
        out_specs=row(NSA_W),
        out_shape=jax.ShapeDtypeStruct((n, NSA_W), BF16),
        compiler_params=_cparams(("parallel", "parallel")),
        name="nsa_sel_win",
    )(qpad, sel, gates, oc, kvb.reshape(batch, seq_len, 2 * KV_COLS))


def _rwprep_kernel(zr_ref, prev_ref, halo_ref, mu_ref, vec_ref, w2_ref, a2_ref, g2_ref, rk_ref,
                   r_o, w_o, k_o, v_o, a_o, b_o, g_o, bonus_o, *, single, tiles_per_seq):
    tm = zr_ref.shape[0]
    zr = zr_ref[...]
    if single:
        prev = prev_ref[...]
    else:
        first_tile = pl.program_id(0) % tiles_per_seq == 0
        before = jnp.where(first_tile, prev_ref[0], halo_ref[7:8, :])
        rowi = lax.broadcasted_iota(jnp.int32, (tm, 1), 0)
        prev = jnp.where(rowi == 0, before, pltpu.roll(zr, 1, 0))
    zs = zr + (prev - zr) * mu_ref[...]
    r = zs[:, 0:RW_W]
    k = zs[:, RW_W:2 * RW_W]
    v = zs[:, 2 * RW_W:3 * RW_W]
    lora = zs[:, 3 * RW_W:3 * RW_W + W_LORA + A_LORA]
    w0, a0, k_k, k_a = (vec_ref[i:i + 1, :] for i in range(4))
    wlog = -jax.nn.softplus(-(w0 + _dot(jnp.tanh(lora), w2_ref[...]))) - 0.5
    log_decay = -jnp.exp(wlog)
    a = jax.nn.sigmoid(a0 + _dot(lora, a2_ref[...]))
    g = _dot(jax.nn.sigmoid(zs[:, 3 * RW_W + W_LORA + A_LORA:]), g2_ref[...])
    kk = k * k_k
    kk = kk / jnp.maximum(jnp.sqrt(_head_sum(kk * kk)), 1e-12)
    k2 = k * (1.0 + (a - 1.0) * k_a)
    r_o[...] = r
    w_o[...] = jnp.exp(log_decay) if single else log_decay
    k_o[...] = k2
    v_o[...] = v
    a_o[...] = -kk
    b_o[...] = kk * a
    g_o[...] = g
    bonus_o[...] = _head_sum(r * k2 * rk_ref[...]) * v


def _rw_prep(zr, shift_prev, P, *, seq_len, tm):
    n = zr.shape[0]
    single = seq_len == 1
    w2p = jnp.concatenate([P['rw_w2'], jnp.zeros((A_LORA, RW_W), F32)], axis=0).astype(BF16)
    a2p = jnp.concatenate([jnp.zeros((W_LORA, RW_W), F32), P['rw_a2']], axis=0).astype(BF16)
    const = lambda shape: pl.BlockSpec(shape, lambda i: (0,) * len(shape))
    if single:
        prev_arg = shift_prev
        prev_spec = pl.BlockSpec((tm, RW_COLS), lambda i: (i, 0))
        tiles = 1
    else:
        tiles = seq_len // tm
        prev_arg = shift_prev[:, None, :]
        prev_spec = pl.BlockSpec((1, 1, RW_COLS), lambda i: (i // tiles, 0, 0))
    halo_spec = pl.BlockSpec((8, RW_COLS), lambda i: (jnp.maximum(i * (tm // 8) - 1, 0), 0))
    row = pl.BlockSpec((tm, RW_W), lambda i: (i, 0))
    return pl.pallas_call(
        functools.partial(_rwprep_kernel, single=single, tiles_per_seq=tiles),
        grid=(n // tm,),
        in_specs=[pl.BlockSpec((tm, RW_COLS), lambda i: (i, 0)), prev_spec, halo_spec,
                  const((1, RW_COLS)), const((6, RW_W)), const((W_LORA + A_LORA, RW_W)),
                  const((W_LORA + A_LORA, RW_W)), const((G_LORA, RW_W)), const((1, RW_W))],
        out_specs=[row] * 8,
        out_shape=[jax.ShapeDtypeStruct((n, RW_W), F32)] * 8,
        compiler_params=_cparams(("parallel",)),
        name="rwkv_prep",
    )(zr, prev_arg, zr, P['rw_mu'][None, :], P['rw_vec'], w2p, a2p, P['rw_g2'].astype(BF16),
      P['rw_rk'].reshape(1, RW_W))


def _segsum(x, ones2):
    hi = x.astype(BF16)
    lo = (x - hi.astype(F32)).astype(BF16)
    return jnp.dot(jnp.concatenate([hi, lo], axis=1), ones2, preferred_element_type=F32)


def _scan_kernel(r_ref, w_ref, k_ref, v_ref, a_ref, b_ref, s0_ref, y_ref, sfin_ref, s_scr):
    nb, tc = r_ref.shape[0], r_ref.shape[1]
    npair = RW_HEADS // 2
    units = [(b, hp) for b in range(nb) for hp in range(npair)]

    @pl.when(pl.program_id(0) == 0)
    def _():
        s_scr[...] = s0_ref[...]

    ri = lax.broadcasted_iota(jnp.int32, (2 * LANES, LANES), 0) % LANES
    ci = lax.broadcasted_iota(jnp.int32, (2 * LANES, LANES), 1)
    ones2 = (ri // HEAD_DIM == ci // HEAD_DIM).astype(BF16)
    diag = (lax.broadcasted_iota(jnp.int32, (HEAD_DIM, LANES), 0)
            == lax.broadcasted_iota(jnp.int32, (HEAD_DIM, LANES), 1) % HEAD_DIM)

    nu = len(units)
    refs = dict(r=r_ref, w=w_ref, k=k_ref, v=v_ref, a=a_ref, b=b_ref)

    def run(base, nsteps):
        blocks = {name: [ref[b, pl.ds(base, nsteps), hp * LANES:(hp + 1) * LANES] for b, hp in units]
                  for name, ref in refs.items()}
        states = [s_scr[b, hp] for b, hp in units]
        yrows = [[] for _ in units]
        for t in range(nsteps):
            row = lambda name, i: blocks[name][i][t:t + 1, :]
            parts = [states[i] * row('a', i) for i in range(nu)]
            parts += [jnp.where(diag, row('v', i), 0.0) for i in range(nu)]
            red = _segsum(jnp.concatenate(parts, axis=0), ones2)
            for i in range(nu):
                sa = red[i * HEAD_DIM:(i + 1) * HEAD_DIM]
                vb = red[(nu + i) * HEAD_DIM:(nu + i + 1) * HEAD_DIM]
                states[i] = states[i] * row('w', i) + sa * row('b', i) + vb * row('k', i)
            yred = _segsum(jnp.concatenate([states[i] * row('r', i) for i in range(nu)], axis=0), ones2)
            for i in range(nu):
                yb = yred[i * HEAD_DIM:(i + 1) * HEAD_DIM]
                yrows[i].append(jnp.sum(jnp.where(diag, yb, 0.0), axis=0, keepdims=True))
        for i, (b, hp) in enumerate(units):
            s_scr[b, hp] = states[i]
            y_ref[b, pl.ds(base, nsteps), hp * LANES:(hp + 1) * LANES] = jnp.concatenate(yrows[i], axis=0)

    if tc % 8 == 0:
        def group(t8, carry):
            run(pl.multiple_of(t8 * 8, 8), 8)
            return carry

        lax.fori_loop(0, tc // 8, group, 0)
    else:
        run(0, tc)

    @pl.when(pl.program_id(0) == pl.num_programs(0) - 1)
    def _():
        sfin_ref[...] = s_scr[...]


def _state_to_pairs(s):
    nb = s.shape[0]
    return s.reshape(nb, 2, 2, HEAD_DIM, HEAD_DIM).transpose(0, 1, 3, 2, 4).reshape(nb, 2, HEAD_DIM, LANES)


def _pairs_to_state(s):
    nb = s.shape[0]
    return s.reshape(nb, 2, HEAD_DIM, 2, HEAD_DIM).transpose(0, 1, 3, 2, 4).reshape(nb, RW_HEADS, HEAD_DIM, HEAD_DIM)


def _rw_scan(r, w, k, v, a, b, s0, *, batch, seq_len, tc):
    args = [t.reshape(batch, seq_len, RW_W) for t in (r, w, k, v, a, b)]
    blk = pl.BlockSpec((batch, tc, RW_W), lambda c: (0, c, 0))
    st = pl.BlockSpec((batch, 2, HEAD_DIM, LANES), lambda c: (0, 0, 0, 0))
    y, sfin = pl.pallas_call(
        _scan_kernel,
        grid=(seq_len // tc,),
        in_specs=[blk] * 6 + [st],
        out_specs=[blk, st],
        out_shape=[jax.ShapeDtypeStruct((batch, seq_len, RW_W), F32),
                   jax.ShapeDtypeStruct((batch, 2, HEAD_DIM, LANES), F32)],
        scratch_shapes=[pltpu.VMEM((batch, 2, HEAD_DIM, LANES), F32)],
        compiler_params=_cparams(("arbitrary",)),
        name="rwkv_scan",
    )(*args, _state_to_pairs(s0))
    return y.reshape(batch * seq_len, RW_W), _pairs_to_state(sfin)


RW_CHUNK = 64


def _block_rows(x):
    c = x.shape[0]
    tiled = jnp.concatenate([x] * RW_HEADS, axis=0)
    same = (lax.broadcasted_iota(jnp.int32, tiled.shape, 0) // c
            == lax.broadcasted_iota(jnp.int32, tiled.shape, 1) // HEAD_DIM)
    return jnp.where(same, tiled, 0.0)


def _fold_rows(x_bd):
    c = x_bd.shape[0] // RW_HEADS
    return functools.reduce(lambda p, q: p + q, [x_bd[h * c:(h + 1) * c] for h in range(RW_HEADS)])


def _rw_chunk_prep_kernel(r_ref, lw_ref, k_ref, v_ref, a_ref, b_ref,
                          at_o, rt_o, tcat_o, brb_o, uv_o, yv_o, bc_o, kc_o, gend_o):
    c = RW_CHUNK
    n4 = RW_HEADS * c
    rowi = lax.broadcasted_iota(jnp.int32, (c, RW_W), 0)
    ti = lax.broadcasted_iota(jnp.int32, (n4, n4), 0)
    ji = lax.broadcasted_iota(jnp.int32, (n4, n4), 1)
    same_head = ti // c == ji // c
    strict = same_head & (ti % c > ji % c)
    incl = same_head & (ti % c >= ji % c)
    eye = (ti == ji).astype(F32)
    gends = []
    for ci in range(r_ref.shape[0] // c):
        rows = slice(ci * c, (ci + 1) * c)
        lw = lw_ref[rows, :]
        cs = lw
        shift = 1
        while shift < c:
            cs = cs + jnp.where(rowi >= shift, pltpu.roll(cs, shift, 0), 0.0)
            shift *= 2
        cs_end = cs[c - 1:c, :]
        gends.append(jnp.exp(cs_end))
        inv = jnp.exp(-cs)
        to_end = jnp.exp(cs_end - cs)
        at = a_ref[rows, :] * jnp.exp(cs - lw)
        rt = r_ref[rows, :] * jnp.exp(cs)
        b, k, v = b_ref[rows, :], k_ref[rows, :], v_ref[rows, :]
        bh4 = jnp.concatenate([b * inv] * RW_HEADS, axis=0)
        kh4 = jnp.concatenate([k * inv] * RW_HEADS, axis=0)
        at_bd, rt_bd, v_bd = _block_rows(at), _block_rows(rt), _block_rows(v)
        a_ab = jnp.where(strict, _dot_nt(at_bd, bh4), 0.0)
        a_ak = jnp.where(strict, _dot_nt(at_bd, kh4), 0.0)
        b_rb = jnp.where(incl, _dot_nt(rt_bd, bh4), 0.0)
        b_rk = jnp.where(incl, _dot_nt(rt_bd, kh4), 0.0)
        pair = same_head & ((ti % c) // 2 == (ji % c) // 2)
        t_inv = eye + jnp.where(pair, a_ab, 0.0)
        s = 2
        while s < c:
            lower_left = (((ti % c) // (2 * s) == (ji % c) // (2 * s))
                          & (((ti % c) // s) % 2 == 1) & (((ji % c) // s) % 2 == 0))
            t_inv = t_inv + _dot(_dot(t_inv, jnp.where(lower_left, a_ab, 0.0)), t_inv)
            s *= 2
        at_o[rows, :] = at.astype(BF16)
        rt_o[rows, :] = rt.astype(BF16)
        tcat_o[rows, :] = _fold_rows(t_inv).astype(BF16)
        brb_o[rows, :] = _fold_rows(b_rb).astype(BF16)
        uv_o[rows, :] = _fold_rows(_dot(t_inv, _dot(a_ak, v_bd)))
        yv_o[rows, :] = _fold_rows(_dot(b_rk, v_bd))
        bc_o[rows, :] = (b * to_end).astype(BF16)
        kc_o[rows, :] = (k * to_end).astype(BF16)
    gend_o[...] = jnp.concatenate(gends, axis=0)


def _rw_chain_kernel(at_ref, rt_ref, tcat_ref, brb_ref, uv_ref, yv_ref, bc_ref, kc_ref, v_ref, gend_ref,
                     y_ref, sfin_ref, s_scr):
    nb, tt = at_ref.shape[0], at_ref.shape[1]
    c = RW_CHUNK

    @pl.when(pl.program_id(0) == 0)
    def _():
        s_scr[...] = jnp.zeros_like(s_scr)

    same_head = (lax.broadcasted_iota(jnp.int32, (RW_W, RW_W), 0) // HEAD_DIM
                 == lax.broadcasted_iota(jnp.int32, (RW_W, RW_W), 1) // HEAD_DIM)
    for ci in range(tt // c):
        rows = slice(ci * c, (ci + 1) * c)
        for b in range(nb):
            state = s_scr[b]
            m1 = _dot_nt(jnp.concatenate([at_ref[b, rows, :], rt_ref[b, rows, :]], axis=0), state)
            u = _dot(tcat_ref[b, rows, :], _block_rows(m1[0:c])) + uv_ref[b, rows, :]
            y_ref[b, rows, :] = m1[c:2 * c] + _dot(brb_ref[b, rows, :], _block_rows(u)) + yv_ref[b, rows, :]
            grow = lax.dot_general(
                jnp.concatenate([u, v_ref[b, rows, :]], axis=0).astype(BF16),
                jnp.concatenate([bc_ref[b, rows, :], kc_ref[b, rows, :]], axis=0),
                (((0,), (0,)), ((), ())), preferred_element_type=F32)
            s_scr[b] = state * gend_ref[b, ci:ci + 1, :] + jnp.where(same_head, grow, 0.0)

    @pl.when(pl.program_id(0) == pl.num_programs(0) - 1)
    def _():
        for b in range(nb):
            for h in range(RW_HEADS):
                sfin_ref[b, h] = s_scr[b, h * HEAD_DIM:(h + 1) * HEAD_DIM, h * HEAD_DIM:(h + 1) * HEAD_DIM]


def _rw_chunked(r, lw, k, v, a, b, *, batch, seq_len, tt):
    n = r.shape[0]
    per_tile = tt // RW_CHUNK
    row = pl.BlockSpec((tt, RW_W), lambda i: (i, 0))
    outs = [BF16, BF16, BF16, BF16, F32, F32, BF16, BF16]
    *prep, gend = pl.pallas_call(
        _rw_chunk_prep_kernel,
        grid=(n // tt,),
        in_specs=[row] * 6,
        out_specs=[row] * 8 + [pl.BlockSpec((per_tile, RW_W), lambda i: (i, 0))],
        out_shape=[jax.ShapeDtypeStruct((n, RW_W), dt) for dt in outs]
        + [jax.ShapeDtypeStruct((n // RW_CHUNK, RW_W), F32)],
        compiler_params=_cparams(("parallel",)),
        name="rwkv_chunk_prep",
    )(r, lw, k, v, a, b)
    seq = lambda t: t.reshape(batch, seq_len, RW_W)
    blk = pl.BlockSpec((batch, tt, RW_W), lambda i: (0, i, 0))
    y, s_fin = pl.pallas_call(
        _rw_chain_kernel,
        grid=(seq_len // tt,),
        in_specs=[blk] * 9 + [pl.BlockSpec((batch, per_tile, RW_W), lambda i: (0, i, 0))],
        out_specs=[blk, pl.BlockSpec((batch, RW_HEADS, HEAD_DIM, HEAD_DIM), lambda i: (0, 0, 0, 0))],
        out_shape=[jax.ShapeDtypeStruct((batch, seq_len, RW_W), F32),
                   jax.ShapeDtypeStruct((batch, RW_HEADS, HEAD_DIM, HEAD_DIM), F32)],
        scratch_shapes=[pltpu.VMEM((batch, RW_W, RW_W), F32)],
        compiler_params=_cparams(("arbitrary",)),
        name="rwkv_chain",
    )(*[seq(t) for t in prep], seq(v), gend.reshape(batch, seq_len // RW_CHUNK, RW_W))
    return y.reshape(n, RW_W), s_fin


def _mid_kernel(x_ref, ogm_ref, onsa_ref, y_ref, g_ref, bonus_ref, ng_ref, ln_ref, wo_ref,
                wq_ref, kv_ref, wmo_ref, out_ref):
    tm = x_ref.shape[-2]
    rows = max(tm, 8)
    ld = lambda ref: jnp.broadcast_to(ref[...].reshape(tm, ref.shape[-1]), (rows, ref.shape[-1]))
    x = ld(x_ref)
    y = ld(y_ref)
    ym = _head_sum(y) * (1.0 / HEAD_DIM)
    yc = y - ym
    yv = _head_sum(yc * yc) * (1.0 / HEAD_DIM)
    yn = yc * lax.rsqrt(yv + RW_LN_EPS) * ln_ref[0:1, :] + ln_ref[1:2, :]
    o_rw = (yn + ld(bonus_ref)) * ld(g_ref)
    mix = (_dot(ld(ogm_ref), wo_ref[0:GM_W, :]) + _dot(ld(onsa_ref), wo_ref[GM_W:GM_W + NSA_W, :])
           + _dot(o_rw, wo_ref[GM_W + NSA_W:, :]))
    x = x + _rms(mix, ng_ref[1:2, :])
    q = _dot(_rms(x, ng_ref[2:3, :]), wq_ref[...]) * (MEM_HEAD_DIM ** -0.5)
    kv = kv_ref[0]
    heads = []
    for hh in range(MEM_HEADS):
        lo, hi = hh * MEM_HEAD_DIM, (hh + 1) * MEM_HEAD_DIM
        s = _dot_nt(q[:, lo:hi], kv[:, lo:hi])
        e = jnp.exp(s - jnp.max(s, axis=-1, keepdims=True))
        p = e / jnp.sum(e, axis=-1, keepdims=True)
        heads.append(_dot(p, kv[:, D_MODEL + lo:D_MODEL + hi]))
    o = _dot(jnp.concatenate(heads, axis=1), wmo_ref[...])
    x = x + _rms(o, ng_ref[3:4, :])
    out_ref[...] = x[0:tm].reshape(out_ref.shape)


def _mid(x, ogm, onsa, y, g, bonus, kv_mem, P, *, batch, seq_len, tm):
    n = x.shape[0]
    tiles = seq_len // tm
    if tm >= 8:
        row = lambda w: pl.BlockSpec((tm, w), lambda i: (i, 0))
        shp = lambda t: t
        out_shape = jax.ShapeDtypeStruct((n, D_MODEL), F32)
    else:
        row = lambda w: pl.BlockSpec((1, tm, w), lambda i: (i, 0, 0))
        shp = lambda t: t.reshape(n // tm, tm, t.shape[-1])
        out_shape = jax.ShapeDtypeStruct((n // tm, tm, D_MODEL), F32)
    const = lambda shape: pl.BlockSpec(shape, lambda i: (0,) * len(shape))
    out = pl.pallas_call(
        _mid_kernel,
        grid=(n // tm,),
        in_specs=[row(D_MODEL), row(GM_W), row(NSA_W), row(RW_W), row(RW_W), row(RW_W),
                  const((6, D_MODEL)), const((2, RW_W)), const((D_MODEL, D_MODEL)),
                  const((D_MODEL, D_MODEL)),
                  pl.BlockSpec((1, N_MEM, 2 * D_MODEL), lambda i: (i // tiles, 0, 0)),
                  const((D_MODEL, D_MODEL))],
        out_specs=row(D_MODEL),
        out_shape=out_shape,
        compiler_params=_cparams(("parallel",)),
        name="mix_out_mem_attn",
    )(shp(x), shp(ogm), shp(onsa), shp(y), shp(g), shp(bonus), P['norm_g'], P['rw_vec'][4:6],
      P['w_out'].astype(BF16), P['w_mem_q'].astype(BF16), kv_mem, P['w_mem_o'].astype(BF16))
    return out.reshape(n, D_MODEL)


def _memkv_kernel(x_ref, g_ref, w_ref, o_ref, ob_ref):
    o = _dot(_rms(x_ref[...], g_ref[...]), w_ref[...])
    o_ref[...] = o
    ob_ref[...] = o.astype(BF16)


def _mem_kv(mem, g, w_kv, *, tm):
    n = mem.shape[0]
    return pl.pallas_call(
        _memkv_kernel,
        grid=(n // tm,),
        in_specs=[pl.BlockSpec((tm, D_MODEL), lambda i: (i, 0)), pl.BlockSpec((1, D_MODEL), lambda i: (0, 0)),
                  pl.BlockSpec((D_MODEL, 2 * D_MODEL), lambda i: (0, 0))],
        out_specs=[pl.BlockSpec((tm, 2 * D_MODEL), lambda i: (i, 0))] * 2,
        out_shape=[jax.ShapeDtypeStruct((n, 2 * D_MODEL), F32), jax.ShapeDtypeStruct((n, 2 * D_MODEL), BF16)],
        compiler_params=_cparams(("parallel",)),
        name="mem_kv_proj",
    )(mem, g[None, :], w_kv.astype(BF16))


def _ffn_kernel(x_ref, prev_ref, prev1_ref, ng_ref, wg_ref, wu_ref, cw_ref, cb_ref, wo_ref,
                out_ref, tail_ref, h_scr, acc_scr, tail_scr, *, single, tiles_per_seq):
    tm = x_ref.shape[0]
    i, j = pl.program_id(0), pl.program_id(1)

    @pl.when(j == 0)
    def _():
        h_scr[...] = _rms(x_ref[...], ng_ref[4:5, :]).astype(BF16)
        acc_scr[...] = jnp.zeros_like(acc_scr)

    h = h_scr[...]
    gate = jnp.dot(h, wg_ref[...], preferred_element_type=F32)
    up = jnp.dot(h, wu_ref[...], preferred_element_type=F32)
    if single:
        g2 = prev_ref[...]
        g1 = prev1_ref[...]
        tail_ref[...] = gate
    else:
        first = i % tiles_per_seq == 0
        t0 = jnp.where(first, prev_ref[0, 0:1, :], tail_scr[j, 6:7, :])
        t1 = jnp.where(first, prev_ref[0, 1:2, :], tail_scr[j, 7:8, :])
        rowi = lax.broadcasted_iota(jnp.int32, (tm, 1), 0)
        g1 = jnp.where(rowi == 0, t1, pltpu.roll(gate, 1, 0))
        g2 = jnp.where(rowi == 0, t0, jnp.where(rowi == 1, t1, pltpu.roll(gate, 2, 0)))
        tail_scr[j] = gate[tm - 8:tm]
        tail_ref[0] = gate[tm - 2:tm]
    conv = cb_ref[...] + g2 * cw_ref[0:1, :] + g1 * cw_ref[1:2, :] + gate * cw_ref[2:3, :]
    act = jax.nn.silu(conv) * up
    acc_scr[...] += _dot(act, wo_ref[...])

    @pl.when(j == pl.num_programs(1) - 1)
    def _():
        out_ref[...] = x_ref[...] + _rms(acc_scr[...], ng_ref[5:6, :])


def _ffn(x, conv_prev, P, *, seq_len, tm, tf):
    n = x.shape[0]
    single = seq_len == 1
    nj = D_FF // tf
    w_in = P['ffn_w_in'].astype(BF16)
    if single:
        prev_args = (conv_prev[:, 0], conv_prev[:, 1])
        prev_specs = [pl.BlockSpec((tm, tf), lambda i, j: (i, j))] * 2
        tail_spec = pl.BlockSpec((tm, tf), lambda i, j: (i, j))
        tail_shape = jax.ShapeDtypeStruct((n, D_FF), F32)
        tiles = 1
    else:
        tiles = seq_len // tm
        prev_args = (conv_prev, conv_prev)
        prev_specs = [pl.BlockSpec((1, CONV_W - 1, tf), lambda i, j: (i // tiles, 0, j))] * 2
        tail_spec = pl.BlockSpec((1, CONV_W - 1, tf), lambda i, j: (i, 0, j))
        tail_shape = jax.ShapeDtypeStruct((n // tm, CONV_W - 1, D_FF), F32)
    out, tail = pl.pallas_call(
        functools.partial(_ffn_kernel, single=single, tiles_per_seq=tiles),
        grid=(n // tm, nj),
        in_specs=[pl.BlockSpec((tm, D_MODEL), lambda i, j: (i, 0)), *prev_specs,
                  pl.BlockSpec((6, D_MODEL), lambda i, j: (0, 0)),
                  pl.BlockSpec((D_MODEL, tf), lambda i, j: (0, j)),
                  pl.BlockSpec((D_MODEL, tf), lambda i, j: (0, nj + j)),
                  pl.BlockSpec((CONV_W, tf), lambda i, j: (0, j)),
                  pl.BlockSpec((1, tf), lambda i, j: (0, j)),
                  pl.BlockSpec((tf, D_MODEL), lambda i, j: (j, 0))],
        out_specs=[pl.BlockSpec((tm, D_MODEL), lambda i, j: (i, 0)), tail_spec],
        out_shape=[jax.ShapeDtypeStruct((n, D_MODEL), F32), tail_shape],
        scratch_shapes=[pltpu.VMEM((tm, D_MODEL), BF16), pltpu.VMEM((tm, D_MODEL), F32),
                        pltpu.VMEM((nj, 8, tf), F32)],
        compiler_params=_cparams(("arbitrary", "arbitrary")),
        name="conv_ffn",
    )(x, *prev_args, P['norm_g'], w_in, w_in, P['ffn_conv_w'], P['ffn_conv_b'][None, :],
      P['ffn_w_out'].astype(BF16))
    if single:
        tail = jnp.stack([conv_prev[:, 1], tail], axis=1)
    else:
        tail = tail[tiles - 1::tiles]
    return out, tail


def _sample_cmp_win_kernel(pt_ref, qpad_ref, kvc_hbm, win_ref, kvw_ref, oc_ref, ow_ref, idx_ref,
                           kbuf, sem):
    b = pl.program_id(0)
    n_pages = kbuf.shape[0]
    per_page = PAGE_SIZE // CMP_BLOCK

    def page_copy(p):
        return pltpu.make_async_copy(kvc_hbm.at[pl.ds(pt_ref[b, p], 1)], kbuf.at[pl.ds(p, 1)], sem)

    def start(p, c):
        page_copy(p).start()
        return c

    def wait(p, c):
        page_copy(p).wait()
        return c

    lax.fori_loop(0, n_pages, start, 0)
    lax.fori_loop(0, n_pages, wait, 0)
    q8 = jnp.concatenate([qpad_ref[0, :, h * LANES:(h + 1) * LANES].astype(F32) for h in range(NSA_HEADS)],
                         axis=0)
    ss = [_dot_nt(q8, kbuf[:, j * KV_COLS:j * KV_COLS + LANES]) for j in range(per_page)]
    mx = functools.reduce(jnp.maximum, [jnp.max(s, axis=-1, keepdims=True) for s in ss])
    es = [jnp.exp(s - mx) for s in ss]
    den = functools.reduce(lambda x, y: x + y, [jnp.sum(e, axis=-1, keepdims=True) for e in es])
    ps = [e / den for e in es]
    oc = functools.reduce(lambda x, y: x + y,
                          [_dot(ps[j], kbuf[:, j * KV_COLS + LANES:(j + 1) * KV_COLS]) for j in range(per_page)])
    group0 = lax.broadcasted_iota(jnp.int32, (NSA_HEADS, HEAD_DIM), 0) < NSA_REP
    own_half = lambda o: jnp.where(group0, o[:, 0:HEAD_DIM], o[:, HEAD_DIM:LANES])
    oc_ref[0] = own_half(oc)
    lane = lax.broadcasted_iota(jnp.int32, (NSA_KV_HEADS, n_pages), 1)
    ns_past = 2 * n_pages
    vals = []
    for jj in range(2):
        pj = ps[2 * jj] + ps[2 * jj + 1]
        rows = [jnp.sum(pj[g * NSA_REP:(g + 1) * NSA_REP], axis=0, keepdims=True) for g in range(NSA_KV_HEADS)]
        val = jnp.concatenate(rows, axis=0)
        n_of = 2 * lane + jj
        vals.append(jnp.where(n_of == 0, FORCE, val))
    cur_val = jnp.full((NSA_KV_HEADS, 1), FORCE, F32)
    big = ns_past + 1
    out_lane = lax.broadcasted_iota(jnp.int32, (NSA_KV_HEADS, LANES), 1)
    idx = jnp.zeros((NSA_KV_HEADS, LANES), jnp.int32)
    for rnd in range(TOP_N):
        mx = jnp.maximum(jnp.maximum(jnp.max(vals[0], axis=-1, keepdims=True),
                                     jnp.max(vals[1], axis=-1, keepdims=True)), cur_val)
        c0 = jnp.min(jnp.where(vals[0] == mx, 2 * lane, big), axis=-1, keepdims=True)
        c1 = jnp.min(jnp.where(vals[1] == mx, 2 * lane + 1, big), axis=-1, keepdims=True)
        c2 = jnp.where(cur_val == mx, ns_past, big)
        pick = jnp.minimum(jnp.minimum(c0, c1), c2)
        idx = jnp.where(out_lane == rnd, pick, idx)
        vals[0] = jnp.where(2 * lane == pick, -jnp.inf, vals[0])
        vals[1] = jnp.where(2 * lane + 1 == pick, -jnp.inf, vals[1])
        cur_val = jnp.where(pick == ns_past, -jnp.inf, cur_val)
    idx_ref[0] = jnp.concatenate([idx, jnp.zeros((8 - NSA_KV_HEADS, LANES), jnp.int32)], axis=0)
    wb = win_ref.shape[1]
    s_old = _dot_nt(q8, win_ref[0, :, 0:LANES])
    new_k = jnp.broadcast_to(kvw_ref[0, :, 0:LANES], (8, LANES))
    s_new = _dot_nt(q8, new_k)[:, 0:1]
    keep = lax.broadcasted_iota(jnp.int32, (1, wb), 1) > wb - WINDOW
    s_old = jnp.where(keep, s_old, NEG)
    mx = jnp.maximum(jnp.max(s_old, axis=-1, keepdims=True), s_new)
    e_old = jnp.exp(s_old - mx)
    e_new = jnp.exp(s_new - mx)
    den = jnp.sum(e_old, axis=-1, keepdims=True) + e_new
    new_v = kvw_ref[0, :, LANES:KV_COLS].astype(BF16).astype(F32)
    ow_ref[0] = own_half(_dot(e_old / den, win_ref[0, :, LANES:KV_COLS])
                         + (e_new / den).astype(BF16).astype(F32) * new_v)


def _sample_cmp_win(page_table, qpad, kvc_phys, win_buf, kvw):
    nb, n_pages = page_table.shape
    wb = win_buf.shape[1]
    row_w = (PAGE_SIZE // CMP_BLOCK) * KV_COLS
    piece = jax.ShapeDtypeStruct((nb, NSA_HEADS, HEAD_DIM), F32)
    grid_spec = pltpu.PrefetchScalarGridSpec(
        num_scalar_prefetch=1,
        grid=(nb,),
        in_specs=[pl.BlockSpec((1, 1, NSA_HEADS * LANES), lambda b, pt: (b, 0, 0)),
                  pl.BlockSpec(memory_space=pl.ANY),
                  pl.BlockSpec((1, wb, KV_COLS), lambda b, pt: (b, 0, 0)),
                  pl.BlockSpec((1, 1, KV_COLS), lambda b, pt: (b, 0, 0))],
        out_specs=[pl.BlockSpec((1, NSA_HEADS, HEAD_DIM), lambda b, pt: (b, 0, 0))] * 2
        + [pl.BlockSpec((1, 8, LANES), lambda b, pt: (b, 0, 0))],
        scratch_shapes=[pltpu.VMEM((n_pages, row_w), F32), pltpu.SemaphoreType.DMA(())],
    )
    return pl.pallas_call(
        _sample_cmp_win_kernel,
        grid_spec=grid_spec,
        out_shape=[piece, piece, jax.ShapeDtypeStruct((nb, 8, LANES), jnp.int32)],
        compiler_params=_cparams(("arbitrary",)),
        name="nsa_sample_cmp_win",
    )(page_table, qpad.reshape(nb, 1, NSA_HEADS * LANES), kvc_phys.reshape(-1, row_w), win_buf,
      kvw.reshape(nb, 1, KV_COLS))


def _sample_sel_kernel(pt_ref, idx_ref, q_ref, blk_ref, kvs_ref, gate_ref, oc_ref, ow_ref, out_ref,
                       m_scr, l_scr, acc_scr, *, n_pages):
    b, g, n = pl.program_id(0), pl.program_id(1), pl.program_id(2)

    @pl.when(n == 0)
    def _():
        m_scr[...] = jnp.full_like(m_scr, NEG)
        l_scr[...] = jnp.zeros_like(l_scr)
        acc_scr[...] = jnp.zeros_like(acc_scr)

    q8 = jnp.concatenate([q_ref[0, 0], jnp.zeros((8 - NSA_REP, HEAD_DIM), F32)], axis=0)
    blk = idx_ref[b, g * TOP_N + n]
    is_cur = blk >= 2 * n_pages
    half = blk % (PAGE_SIZE // SLC_BLOCK)
    kv_new = kvs_ref[0]
    k_new = jnp.where(g == 0, kv_new[0:1], kv_new[1:2])
    v_new = jnp.where(g == 0, kv_new[2:3], kv_new[3:4])
    k_t = blk_ref[0, 0, 0, 0]
    v_t = blk_ref[0, 0, 1, 0]
    s_page = _dot(q8, k_t)
    s_new = _dot_nt(q8, jnp.broadcast_to(k_new, (8, HEAD_DIM)))[:, 0:1]
    lane = lax.broadcasted_iota(jnp.int32, (8, PAGE_SIZE), 1)
    s = jnp.where(is_cur, jnp.where(lane == 0, s_new, NEG), jnp.where(lane // SLC_BLOCK == half, s_page, NEG))
    m_old = m_scr[...]
    m_new = jnp.maximum(m_old, jnp.max(s, axis=-1, keepdims=True))
    alpha = jnp.exp(m_old - m_new)
    p = jnp.exp(s - m_new[:, 0:1])
    pv_new = p[:, 0:1].astype(BF16).astype(F32) * v_new.astype(BF16).astype(F32)
    pv = jnp.where(is_cur, pv_new, _dot_nt(p, v_t))
    l_scr[...] = alpha * l_scr[...] + jnp.sum(p, axis=-1, keepdims=True)
    acc_scr[...] = alpha[:, 0:HEAD_DIM] * acc_scr[...] + pv
    m_scr[...] = m_new

    @pl.when(n == pl.num_programs(2) - 1)
    def _():
        o_s = (acc_scr[...] / l_scr[:, 0:HEAD_DIM])[0:NSA_REP]
        head = g * NSA_REP + lax.broadcasted_iota(jnp.int32, (NSA_REP, GATE_PAD), 0)
        lane_g = lax.broadcasted_iota(jnp.int32, (NSA_REP, GATE_PAD), 1)
        gate = jnp.broadcast_to(gate_ref[0], (NSA_REP, GATE_PAD))
        gcol = lambda br: jnp.sum(jnp.where(lane_g == br * NSA_HEADS + head, gate, 0.0), axis=-1, keepdims=True)
        out_ref[0, 0] = gcol(0) * oc_ref[0, 0] + gcol(1) * o_s + gcol(2) * ow_ref[0, 0]


def _sample_sel(page_table, idx, q, pool_pages, layer, kvs, gates, oc, ow):
    nb, n_pages = page_table.shape
    halves = PAGE_SIZE // SLC_BLOCK

    def blk_map(b, g, n, pt, ix):
        page = jnp.minimum(ix[b, g * TOP_N + n] // halves, n_pages - 1)
        return (layer, pt[b, page], 0, g, 0, 0)

    per_bg = pl.BlockSpec((1, 1, NSA_REP, HEAD_DIM), lambda b, g, n, pt, ix: (b, g, 0, 0))
    per_b = lambda shape: pl.BlockSpec((1,) + shape, lambda b, g, n, pt, ix: (b, 0, 0))
    grid_spec = pltpu.PrefetchScalarGridSpec(
        num_scalar_prefetch=2,
        grid=(nb, NSA_KV_HEADS, TOP_N),
        in_specs=[per_bg, pl.BlockSpec((1, 1, 2, 1, HEAD_DIM, PAGE_SIZE), blk_map),
                  per_b((2 * NSA_KV_HEADS, HEAD_DIM)), per_b((1, GATE_PAD)), per_bg, per_bg],
        out_specs=per_bg,
        scratch_shapes=[pltpu.VMEM((8, LANES), F32), pltpu.VMEM((8, LANES), F32),
                        pltpu.VMEM((8, HEAD_DIM), F32)],
    )
    grouped = lambda t: t.reshape(nb, NSA_KV_HEADS, NSA_REP, HEAD_DIM)
    out = pl.pallas_call(
        functools.partial(_sample_sel_kernel, n_pages=n_pages),
        grid_spec=grid_spec,
        out_shape=jax.ShapeDtypeStruct((nb, NSA_KV_HEADS, NSA_REP, HEAD_DIM), F32),
        compiler_params=_cparams(("arbitrary", "arbitrary", "arbitrary")),
        name="nsa_sample_sel",
    )(page_table, idx, grouped(q), pool_pages, kvs.reshape(nb, 2 * NSA_KV_HEADS, HEAD_DIM),
      gates.reshape(nb, 1, GATE_PAD), grouped(oc), grouped(ow))
    return out.reshape(nb, NSA_W)


def _tile(n, want):
    t = min(n, want)
    while n % t or (t % 8 and t != n):
        t -= 1
    return t


def _prompt_layer(x, tabs, mem_prompt, P, *, batch, seq_len):
    kv_f32, kv_b16 = _mem_kv(mem_prompt, P['mem_g'], P['w_mem_kv'], tm=_tile(mem_prompt.shape[0], 256))
    w_pack = _pack_w_in(P['w_in'])
    ogm, _, qpad, kvc, kvs, kvw, kvb, gates, zr = _in_proj(
        x, P['norm_g'][0], w_pack, tabs[0], tabs[1], P['gm_ln'], P['gm_ws'], P['gm_bs'],
        seq_len=seq_len, tm=_tile(seq_len, 256))
    pe, wb = _compress_weights(P['cmp_pe'], P['cmp_w'])
    nc = seq_len // CMP_BLOCK
    comp = _compress(kvc, pe, wb, tr=seq_len)
    comp = comp.reshape(batch, nc // 2, 2, KV_COLS).transpose(0, 2, 1, 3).reshape(batch, nc, KV_COLS)
    oc, sel = _cmp_attn(qpad, comp, batch=batch, seq_len=seq_len, tq=_tile(seq_len, 256))
    onsa = _sel_win(qpad, sel, gates, oc, kvb, batch=batch, seq_len=seq_len, tq=128)
    shift0 = jnp.zeros((batch, RW_COLS), F32)
    r, lw, k, v, a, b, g, bonus = _rw_prep(zr, shift0, P, seq_len=seq_len, tm=_tile(seq_len, 512))
    y, s_fin = _rw_chunked(r, lw, k, v, a, b, batch=batch, seq_len=seq_len, tt=_tile(seq_len, 512))
    x = _mid(x, ogm, onsa, y, g, bonus, kv_b16.reshape(batch, N_MEM, 2 * D_MODEL), P,
             batch=batch, seq_len=seq_len, tm=_tile(seq_len, 256))
    conv0 = jnp.zeros((batch, CONV_W - 1, D_FF), F32)
    x, conv_new = _ffn(x, conv0, P, seq_len=seq_len, tm=_tile(seq_len, 512), tf=D_FF // 2)
    kvshape = (batch, seq_len, 2, NSA_KV_HEADS, HEAD_DIM)
    wbp = min(WINDOW, seq_len)
    states = dict(
        cmp=kvc.reshape(kvshape), slc=kvs.reshape(kvshape), win=kvw.reshape(kvshape)[:, seq_len - wbp:],
        rw=s_fin, shift=zr.reshape(batch, seq_len, RW_COLS)[:, -1], conv=conv_new,
        mem=kv_f32.reshape(batch, N_MEM, 2, MEM_HEADS, MEM_HEAD_DIM))
    return x, states


def _sample_layer(x, tabs, page_table, cmp_pages, slc_pages, layer, win_buf, mem_kv, rw_state, shift_prev,
                  conv_prev, P):
    nb = x.shape[0]
    w_pack = _pack_w_in(P['w_in'])
    ogm, vgm, qpad, kvc, kvs, kvw, _, gates, zr, q = _in_proj(
        x, P['norm_g'][0], w_pack, tabs[0], tabs[1], P['gm_ln'], P['gm_ws'], P['gm_bs'], seq_len=1, tm=nb)
    pe, wb = _compress_weights(P['cmp_pe'], P['cmp_w'])
    n_phys = cmp_pages.shape[1]
    kvc_phys = _compress_pages(cmp_pages, layer, pe, wb, pages=_tile(n_phys, 64))
    oc, ow, idx = _sample_cmp_win(page_table, qpad, kvc_phys, win_buf.reshape(nb, -1, KV_COLS), kvw)
    idx = idx[:, :NSA_KV_HEADS, :TOP_N].reshape(nb, NSA_KV_HEADS * TOP_N)
    slc6 = slc_pages.reshape(slc_pages.shape[0], n_phys, 2, NSA_KV_HEADS, HEAD_DIM, PAGE_SIZE)
    onsa = _sample_sel(page_table, idx, q, slc6, layer, kvs, gates, oc, ow)
    r, w, k, v, a, b, g, bonus = _rw_prep(zr, shift_prev, P, seq_len=1, tm=nb)
    y, s_fin = _rw_scan(r, w, k, v, a, b, rw_state, batch=nb, seq_len=1, tc=1)
    x = _mid(x, ogm, onsa, y, g, bonus, mem_kv.reshape(nb, N_MEM, 2 * D_MODEL), P, batch=nb, seq_len=1, tm=1)
    x, conv_new = _ffn(x, conv_prev, P, seq_len=1, tm=nb, tf=256)
    kvshape = (nb, 1, 2, NSA_KV_HEADS, HEAD_DIM)
    states = dict(cmp=kvc.reshape(kvshape), slc=kvs.reshape(kvshape), win=kvw.reshape(kvshape), rw=s_fin,
                  shift=zr, conv=conv_new, gv=vgm.reshape(nb, 1, GM_W))
    return x, states


def kernel(x_prompt, x_sample, cache_cmp_kv, cache_slc_kv, cache_win_kv, cache_mem_kv, state_rwkv, state_rwkv_shift, state_ffn_conv, page_table, mem_prompt, norm_g, w_in, w_out, gm_ln, gm_ws, gm_bs, cmp_pe, cmp_w, rw_mu, rw_vec, rw_w2, rw_a2, rw_g2, rw_rk, mem_g, w_mem_q, w_mem_kv, w_mem_o, ffn_w_in, ffn_conv_w, ffn_conv_b, ffn_w_out):
    bp, tp = x_prompt.shape[:2]
    nb, ts = x_sample.shape[:2]
    assert ts == 1 and tp % 512 == 0
    depth = norm_g.shape[0]
    past = page_table.shape[1] * PAGE_SIZE
    tabs_p = _rope_tables(jnp.arange(tp))
    tabs_s = _rope_tables(jnp.full((nb,), past))
    xp = x_prompt.reshape(bp * tp, D_MODEL)
    xs = x_sample.reshape(nb, D_MODEL)
    mem = mem_prompt.reshape(bp * N_MEM, D_MODEL)
    weights = dict(norm_g=norm_g, w_in=w_in, w_out=w_out, gm_ln=gm_ln, gm_ws=gm_ws, gm_bs=gm_bs,
                   cmp_pe=cmp_pe, cmp_w=cmp_w, rw_mu=rw_mu, rw_vec=rw_vec, rw_w2=rw_w2, rw_a2=rw_a2,
                   rw_g2=rw_g2, rw_rk=rw_rk, mem_g=mem_g, w_mem_q=w_mem_q, w_mem_kv=w_mem_kv,
                   w_mem_o=w_mem_o, ffn_w_in=ffn_w_in, ffn_conv_w=ffn_conv_w, ffn_conv_b=ffn_conv_b,
                   ffn_w_out=ffn_w_out)
    cmp_pages = _pool_pages(cache_cmp_kv)
    slc_pages = _pool_pages(cache_slc_kv)
    ps, ss = [], []
    for l in range(depth):
        P = {name: val[l] for name, val in weights.items()}
        xp, st = _prompt_layer(xp, tabs_p, mem, P, batch=bp, seq_len=tp)
        ps.append(st)
        xs, st = _sample_layer(xs, tabs_s, page_table, cmp_pages, slc_pages, l, cache_win_kv[l],
                               cache_mem_kv[l], state_rwkv[l], state_rwkv_shift[l], state_ffn_conv[l], P)
        ss.append(st)
    stack = lambda sts, name: jnp.stack([st[name] for st in sts])
    return (xp.reshape(bp, tp, D_MODEL), xs.reshape(nb, 1, D_MODEL),
            stack(ps, 'cmp'), stack(ps, 'slc'), stack(ps, 'win'), stack(ps, 'rw'), stack(ps, 'shift'),
            stack(ps, 'conv'), stack(ps, 'mem'),
            stack(ss, 'cmp'), stack(ss, 'slc'), stack(ss, 'win'), stack(ss, 'rw'), stack(ss, 'shift'),
            stack(ss, 'conv'), stack(ss, 'gv'))
```

```python
import functools
import math

import jax
import jax.numpy as jnp
from jax import lax
from jax.experimental import pallas as pl
from jax.experimental.pallas import tpu as pltpu

F32 = jnp.float32
BF16 = jnp.bfloat16

D_MODEL = 1024
HEAD_DIM = 64
GM_W = 256
GM_HEADS = 4
CHUNK = 128
NSA_W = 512
NSA_HEADS = 8
NSA_KV_HEADS = 2
NSA_REP = 4
CMP_BLOCK = 32
SLC_BLOCK = 64
TOP_N = 16
WINDOW = 512
RW_W = 256
RW_HEADS = 4
W_LORA = 64
A_LORA = 64
G_LORA = 128
RW_COLS = 3 * RW_W + W_LORA + A_LORA + G_LORA
RW_LN_EPS = 64e-5
N_MEM = 256
MEM_HEADS = 4
MEM_HEAD_DIM = 256
D_FF = 2816
CONV_W = 3
PAGE_SIZE = 128
ROPE_THETA = 10000.0
NORM_EPS = 1e-6
LN_EPS = 1e-5
NEG = -1e30
FORCE = 1e4
OFF_GM = 0
OFF_Q = 512
OFF_KV = 1024
OFF_GATE = 1792
OFF_RW = 1816
IN_COLS = 2840
KV_COLS = 2 * NSA_KV_HEADS * HEAD_DIM
CMP_K = CMP_BLOCK * KV_COLS
GATE_PAD = 128
PACK_COLS = OFF_GATE + RW_COLS + GATE_PAD

LANES = 128
VMEM_LIMIT = 56 * 1024 * 1024


def _cparams(sem):
    return pltpu.CompilerParams(dimension_semantics=sem, vmem_limit_bytes=VMEM_LIMIT)


def _rms(x, g):
    return x * lax.rsqrt(jnp.mean(x * x, axis=-1, keepdims=True) + NORM_EPS) * g


def _dot(a, b):
    return jnp.dot(a.astype(BF16), b.astype(BF16), preferred_element_type=F32)


def _dot_nt(a, b):
    return lax.dot_general(a.astype(BF16), b.astype(BF16), (((1,), (1,)), ((), ())),
                           preferred_element_type=F32)


def _lane_lo(shape):
    return lax.broadcasted_iota(jnp.int32, shape, len(shape) - 1) % LANES < HEAD_DIM


def _head_sum(x):
    cols = []
    for c in range(x.shape[1] // LANES):
        xc = x[:, c * LANES:(c + 1) * LANES]
        lo_mask = _lane_lo(xc.shape)
        lo = jnp.sum(jnp.where(lo_mask, xc, 0.0), axis=-1, keepdims=True)
        hi = jnp.sum(jnp.where(lo_mask, 0.0, xc), axis=-1, keepdims=True)
        cols.append(jnp.where(lo_mask, lo, hi))
    return jnp.concatenate(cols, axis=1)


def _inproj_kernel(x_ref, g_ref, w_ref, cos_ref, sin_ref, ln_ref, ws_ref, bias_ref,
                   ogm_ref, vgm_ref, qpad_ref, kvc_ref, kvs_ref, kvw_ref, kvb_ref, gate_ref, zr_ref, *q_ref,
                   single):
    tm = x_ref.shape[0]
    h = _rms(x_ref[...], g_ref[...])
    z = _dot(h, w_ref[...])
    zg = jax.nn.gelu(z[:, 0:2 * GM_W])
    u = zg[:, :GM_W]
    vv = zg[:, GM_W:]
    mean = jnp.mean(vv, axis=-1, keepdims=True)
    var = jnp.mean(jnp.square(vv - mean), axis=-1, keepdims=True)
    v = (vv - mean) * lax.rsqrt(var + LN_EPS) * ln_ref[0:1, :] + ln_ref[1:2, :]
    vgm_ref[...] = v
    if single:
        ogm_ref[...] = (u * (v * ws_ref[...] + bias_ref[...])).astype(BF16)
    else:
        head_of_lane = lax.broadcasted_iota(jnp.int32, (CHUNK, GM_W), 1) // HEAD_DIM
        causal = (lax.broadcasted_iota(jnp.int32, (CHUNK, CHUNK), 0)
                  >= lax.broadcasted_iota(jnp.int32, (CHUNK, CHUNK), 1))
        wms = [jnp.where(causal, ws_ref[hh], 0.0).astype(BF16) for hh in range(GM_HEADS)]
        for c in range(tm // CHUNK):
            vc = v[c * CHUNK:(c + 1) * CHUNK]
            acc = bias_ref[...]
            for hh in range(GM_HEADS):
                vm = jnp.where(head_of_lane == hh, vc, 0.0).astype(BF16)
                acc = acc + jnp.dot(wms[hh], vm, preferred_element_type=F32)
            ogm_ref[c * CHUNK:(c + 1) * CHUNK, :] = (u[c * CHUNK:(c + 1) * CHUNK] * acc).astype(BF16)
    cos = cos_ref[...]
    sin = sin_ref[...]
    first_half = lax.broadcasted_iota(jnp.int32, (tm, LANES), 1) % HEAD_DIM < HEAD_DIM // 2
    lane_lo = _lane_lo((tm, LANES))

    def rope(xc):
        rot = jnp.where(first_half, pltpu.roll(xc, LANES - HEAD_DIM // 2, 1), pltpu.roll(xc, HEAD_DIM // 2, 1))
        return xc * cos + rot * sin

    scale = HEAD_DIM ** -0.5
    for c in range(NSA_W // LANES):
        qc = rope(z[:, OFF_Q + c * LANES:OFF_Q + (c + 1) * LANES]) * scale
        if single:
            q_ref[0][:, c * LANES:(c + 1) * LANES] = qc
        qr = pltpu.roll(qc, HEAD_DIM, 1)
        if (2 * c) // NSA_REP == 0:
            even = jnp.where(lane_lo, qc, 0.0)
            odd = jnp.where(lane_lo, qr, 0.0)
        else:
            even = jnp.where(lane_lo, 0.0, qr)
            odd = jnp.where(lane_lo, 0.0, qc)
        qpad_ref[:, (2 * c) * LANES:(2 * c + 1) * LANES] = even.astype(BF16)
        qpad_ref[:, (2 * c + 1) * LANES:(2 * c + 2) * LANES] = odd.astype(BF16)
    for j, ref in enumerate((kvc_ref, kvs_ref, kvw_ref)):
        base = OFF_KV + j * KV_COLS
        kk = rope(z[:, base:base + LANES])
        vj = z[:, base + LANES:base + KV_COLS]
        ref[:, 0:LANES] = kk
        ref[:, LANES:KV_COLS] = vj
        if j >= 1:
            kvb_ref[:, (j - 1) * KV_COLS:(j - 1) * KV_COLS + LANES] = kk.astype(BF16)
            kvb_ref[:, (j - 1) * KV_COLS + LANES:j * KV_COLS] = vj.astype(BF16)
    zr_ref[...] = z[:, OFF_GATE:OFF_GATE + RW_COLS]
    gate_ref[...] = jax.nn.sigmoid(z[:, OFF_GATE + RW_COLS:])


def _pack_w_in(w_in):
    gate = w_in[:, OFF_GATE:OFF_RW].reshape(D_MODEL, NSA_HEADS, 3)
    gate = jnp.transpose(gate, (0, 2, 1)).reshape(D_MODEL, 3 * NSA_HEADS)
    gate = jnp.pad(gate, ((0, 0), (0, GATE_PAD - 3 * NSA_HEADS)))
    return jnp.concatenate([w_in[:, :OFF_GATE], w_in[:, OFF_RW:], gate], axis=1).astype(BF16)


def _in_proj(x, g0, w_pack, cos, sin, gm_ln, gm_ws, gm_bs, *, seq_len, tm):
    n = x.shape[0]
    single = seq_len == 1
    if single:
        ws_arg = jnp.repeat(gm_ws[:, 0, 0], HEAD_DIM)[None, :]
        bias_arg = jnp.repeat(gm_bs[:, 0], HEAD_DIM)[None, :]
        ws_spec = pl.BlockSpec((1, GM_W), lambda i: (0, 0))
        bias_spec = pl.BlockSpec((1, GM_W), lambda i: (0, 0))
        tab_map = lambda i: (i, 0)
    else:
        assert seq_len % tm == 0 and tm % CHUNK == 0
        ws_arg = gm_ws
        bias_arg = jnp.repeat(gm_bs.T, HEAD_DIM, axis=1)
        ws_spec = pl.BlockSpec((GM_HEADS, CHUNK, CHUNK), lambda i: (0, 0, 0))
        bias_spec = pl.BlockSpec((CHUNK, GM_W), lambda i: (0, 0))
        tiles = seq_len // tm
        tab_map = lambda i: (i % tiles, 0)
    row = lambda w: pl.BlockSpec((tm, w), lambda i: (i, 0))
    outs = [(GM_W, BF16), (GM_W, F32), (NSA_HEADS * LANES, BF16), (KV_COLS, F32), (KV_COLS, F32),
            (KV_COLS, F32), (2 * KV_COLS, BF16), (GATE_PAD, F32), (RW_COLS, F32)]
    if single:
        outs.append((NSA_W, F32))
    return pl.pallas_call(
        functools.partial(_inproj_kernel, single=single),
        grid=(n // tm,),
        in_specs=[row(D_MODEL), pl.BlockSpec((1, D_MODEL), lambda i: (0, 0)),
                  pl.BlockSpec((D_MODEL, PACK_COLS), lambda i: (0, 0)),
                  pl.BlockSpec((tm, LANES), tab_map), pl.BlockSpec((tm, LANES), tab_map),
                  pl.BlockSpec((2, GM_W), lambda i: (0, 0)), ws_spec, bias_spec],
        out_specs=[row(w) for w, _ in outs],
        out_shape=[jax.ShapeDtypeStruct((n, w), dt) for w, dt in outs],
        compiler_params=_cparams(("parallel",)),
        name="in_proj",
    )(x, g0[None, :], w_pack, cos, sin, gm_ln, ws_arg, bias_arg)


def _rope_tables(pos):
    half = HEAD_DIM // 2
    inv = ROPE_THETA ** (-jnp.arange(half, dtype=F32) / half)
    ang = pos.astype(F32)[:, None] * inv[None, :]
    cos, sin = jnp.cos(ang), jnp.sin(ang)
    return jnp.tile(cos, (1, 4)), jnp.tile(jnp.concatenate([-sin, sin], axis=1), (1, 2))


def _compress_rows(k_rows_ref, v_rows_ref, pe_ref, w_ref, n_out):
    acc_k = jnp.zeros((n_out, LANES), F32)
    acc_v = jnp.zeros((n_out, LANES), F32)
    for l in range(CMP_BLOCK):
        xk = k_rows_ref[pl.ds(l, n_out, stride=CMP_BLOCK), :] + pe_ref[l:l + 1, 0:LANES]
        xv = v_rows_ref[pl.ds(l, n_out, stride=CMP_BLOCK), :] + pe_ref[l:l + 1, LANES:KV_COLS]
        acc_k = acc_k + _dot(xk, w_ref[l, 0:LANES, 0:LANES])
        acc_v = acc_v + _dot(xv, w_ref[l, LANES:KV_COLS, LANES:KV_COLS])
    return jnp.concatenate([acc_k, acc_v], axis=1)


def _compress_kernel(k_ref, v_ref, pe_ref, w_ref, o_ref):
    o_ref[...] = _compress_rows(k_ref, v_ref, pe_ref, w_ref, o_ref.shape[0]).astype(o_ref.dtype)


def _compress_pages_kernel(x_ref, pe_ref, w_ref, o_ref, k_scr, v_scr):
    def transpose_page(p, c):
        rows = pl.ds(pl.multiple_of(p * PAGE_SIZE, PAGE_SIZE), PAGE_SIZE)
        k_scr[rows, :] = x_ref[0, p, 0:LANES, :].T
        v_scr[rows, :] = x_ref[0, p, LANES:KV_COLS, :].T
        return c

    lax.fori_loop(0, x_ref.shape[1], transpose_page, 0)
    o_ref[...] = _compress_rows(k_scr, v_scr, pe_ref, w_ref, o_ref.shape[0])


def _compress_weights(cmp_pe, cmp_w):
    eye = jnp.eye(2 * NSA_KV_HEADS, dtype=F32).reshape(2, NSA_KV_HEADS, 2, NSA_KV_HEADS)
    wb = jnp.einsum('lsde,sgtq->lsgdtqe', cmp_w, eye).reshape(CMP_BLOCK, KV_COLS, KV_COLS).astype(BF16)
    pe = jnp.broadcast_to(cmp_pe[:, :, None, :], (CMP_BLOCK, 2, NSA_KV_HEADS, HEAD_DIM)).reshape(CMP_BLOCK, KV_COLS)
    return pe, wb


def _cmp_const_specs():
    return [pl.BlockSpec((CMP_BLOCK, KV_COLS), lambda i: (0, 0)),
            pl.BlockSpec((CMP_BLOCK, KV_COLS, KV_COLS), lambda i: (0, 0, 0))]


def _compress(x, pe, wb, *, tr):
    n = x.shape[0]
    return pl.pallas_call(
        _compress_kernel,
        grid=(n // tr,),
        in_specs=[pl.BlockSpec((tr, LANES), lambda i: (i, 0)), pl.BlockSpec((tr, LANES), lambda i: (i, 1))]
        + _cmp_const_specs(),
        out_specs=pl.BlockSpec((tr // CMP_BLOCK, KV_COLS), lambda i: (i, 0)),
        out_shape=jax.ShapeDtypeStruct((n // CMP_BLOCK, KV_COLS), BF16),
        compiler_params=_cparams(("parallel",)),
        name="nsa_compress",
    )(x, x, pe, wb)


def _pool_pages(pool):
    depth, n_phys = pool.shape[:2]
    return jnp.transpose(pool, (0, 1, 3, 4, 5, 2)).reshape(depth, n_phys, KV_COLS, PAGE_SIZE)


def _compress_pages(pool_pages, layer, pe, wb, *, pages):
    n_phys = pool_pages.shape[1]
    per_page = PAGE_SIZE // CMP_BLOCK
    return pl.pallas_call(
        _compress_pages_kernel,
        grid=(n_phys // pages,),
        in_specs=[pl.BlockSpec((1, pages, KV_COLS, PAGE_SIZE), lambda i: (layer, i, 0, 0))] + _cmp_const_specs(),
        out_specs=pl.BlockSpec((pages * per_page, KV_COLS), lambda i: (i, 0)),
        out_shape=jax.ShapeDtypeStruct((n_phys * per_page, KV_COLS), F32),
        scratch_shapes=[pltpu.VMEM((pages * PAGE_SIZE, LANES), F32)] * 2,
        compiler_params=_cparams(("parallel",)),
        name="nsa_compress_pages",
    )(pool_pages, pe, wb)


def _group_q(qpad_ref, g):
    return jnp.concatenate([qpad_ref[:, (g * NSA_REP + r) * LANES:(g * NSA_REP + r + 1) * LANES]
                            for r in range(NSA_REP)], axis=0)


def _assemble_heads(pieces, tq):
    lane_lo = _lane_lo((tq, LANES))
    cols = []
    for c in range(NSA_W // LANES):
        g = (2 * c) // NSA_REP
        r = (2 * c) % NSA_REP
        even = pieces[g][r * tq:(r + 1) * tq]
        odd = pieces[g][(r + 1) * tq:(r + 2) * tq]
        if g == 0:
            cols.append(jnp.where(lane_lo, even, pltpu.roll(odd, HEAD_DIM, 1)))
        else:
            cols.append(jnp.where(lane_lo, pltpu.roll(even, HEAD_DIM, 1), odd))
    return cols


def _cmp_attn_kernel(qpad_ref, kvc_ref, oc_ref, sel_ref):
    tq = qpad_ref.shape[0]
    q0 = pl.program_id(1) * tq
    kvc = kvc_ref[0]
    nc = kvc.shape[0]
    ns = nc // 2
    coli = lax.broadcasted_iota(jnp.int32, (1, nc), 1)
    cblk = jnp.where(coli < ns, 2 * coli, 2 * (coli - ns) + 1)
    c_end = cblk * CMP_BLOCK + (CMP_BLOCK - 1)
    rowpos = q0 + lax.broadcasted_iota(jnp.int32, (NSA_REP * tq, 1), 0) % tq
    m_c = c_end <= rowpos
    blk = lax.broadcasted_iota(jnp.int32, (ns, tq), 0)
    cur = (q0 + lax.broadcasted_iota(jnp.int32, (ns, tq), 1)) // SLC_BLOCK
    pieces = []
    for g in range(NSA_KV_HEADS):
        qg = _group_q(qpad_ref, g)
        s = _dot_nt(qg, kvc[:, 0:LANES])
        sm = jnp.where(m_c, s, NEG)
        e = jnp.exp(sm - jnp.max(sm, axis=-1, keepdims=True))
        p = jnp.where(m_c, e / jnp.sum(e, axis=-1, keepdims=True), 0.0)
        pieces.append(_dot(p, kvc[:, LANES:KV_COLS]))
        ps = p[0:tq] + p[tq:2 * tq] + p[2 * tq:3 * tq] + p[3 * tq:4 * tq]
        imp = (ps[:, :ns] + ps[:, ns:]).T
        imp = jnp.where((blk == cur) | (blk == 0), FORCE, imp)
        imp = jnp.where(blk > cur, NEG, imp)
        work = imp
        sel = jnp.zeros((ns, tq), F32)
        for _ in range(min(TOP_N, ns)):
            mx = jnp.max(work, axis=0, keepdims=True)
            first = jnp.min(jnp.where(work == mx, blk, ns), axis=0, keepdims=True)
            pick = blk == first
            sel = jnp.where(pick, 1.0, sel)
            work = jnp.where(pick, -jnp.inf, work)
        sel = jnp.where(imp > 0.5 * NEG, sel, 0.0)
        sel_ref[:, g * ns:(g + 1) * ns] = sel.T.astype(BF16)
    cols = _assemble_heads(pieces, tq)
    for c in range(NSA_W // LANES):
        oc_ref[:, c * LANES:(c + 1) * LANES] = cols[c]


def _cmp_attn(qpad, kvc_perm, *, batch, seq_len, tq):
    n = qpad.shape[0]
    nq = seq_len // tq
    nc = kvc_perm.shape[1]
    return pl.pallas_call(
        _cmp_attn_kernel,
        grid=(batch, nq),
        in_specs=[pl.BlockSpec((tq, NSA_HEADS * LANES), lambda b, i: (b * nq + i, 0)),
                  pl.BlockSpec((1, nc, KV_COLS), lambda b, i: (b, 0, 0))],
        out_specs=[pl.BlockSpec((tq, NSA_W), lambda b, i: (b * nq + i, 0)),
                   pl.BlockSpec((tq, nc), lambda b, i: (b * nq + i, 0))],
        out_shape=[jax.ShapeDtypeStruct((n, NSA_W), F32), jax.ShapeDtypeStruct((n, nc), BF16)],
        compiler_params=_cparams(("parallel", "parallel")),
        name="nsa_cmp_attn",
    )(qpad, kvc_perm)


SEL_KV_TILE = 512


def _gate_col(gate, br, c, tq):
    lane_lo = _lane_lo((tq, LANES))
    i0 = br * NSA_HEADS + 2 * c
    return jnp.where(lane_lo, gate[:, i0:i0 + 1], gate[:, i0 + 1:i0 + 2])


def _sel_win_kernel(qpad_ref, sel_ref, gate_ref, oc_ref, kvb_ref, out_ref, *, seq_len):
    tq = qpad_ref.shape[0]
    rows = NSA_REP * tq
    kt = min(SEL_KV_TILE, seq_len)
    ns = seq_len // SLC_BLOCK
    q0 = pl.program_id(1) * tq
    assert kt % tq == 0
    j_diag = q0 // kt
    key_blk = lax.broadcasted_iota(jnp.int32, (kt, ns), 0) // SLC_BLOCK
    blk_lane = lax.broadcasted_iota(jnp.int32, (kt, ns), 1)
    lane_kt = lax.broadcasted_iota(jnp.int32, (1, kt), 1)
    n_win = (WINDOW + tq) // LANES
    all_rows = NSA_KV_HEADS * rows
    rowpos = q0 + lax.broadcasted_iota(jnp.int32, (all_rows, 1), 0) % tq
    qs, unsels = [], []
    for g in range(NSA_KV_HEADS):
        unsel = (1.0 - sel_ref[:, g * ns:(g + 1) * ns].astype(F32)).astype(BF16)
        qs.append(_group_q(qpad_ref, g))
        unsels.extend([unsel] * NSA_REP)
    q_all = jnp.concatenate(qs, axis=0)
    lhs = jnp.concatenate([q_all, jnp.concatenate(unsels, axis=0)], axis=1)
    ones = jnp.ones((1, LANES), BF16)

    def with_ones(v):
        return jnp.concatenate([v, jnp.broadcast_to(ones, v.shape)], axis=1)

    def scores(j):
        start = pl.multiple_of(j * kt, kt)
        k = kvb_ref[0, pl.ds(start, kt), 0:LANES]
        penalty = jnp.where(blk_lane == key_blk + j * (kt // SLC_BLOCK), NEG, 0.0).astype(BF16)
        return _dot_nt(lhs, jnp.concatenate([k, penalty], axis=1))

    def absorb(j, s, m, acc):
        v = kvb_ref[0, pl.ds(pl.multiple_of(j * kt, kt), kt), LANES:KV_COLS]
        m_new = jnp.maximum(m, jnp.max(s, axis=-1, keepdims=True))
        p = jnp.exp(s - m_new).astype(BF16)
        return m_new, jnp.exp(m - m_new) * acc + jnp.dot(p, with_ones(v), preferred_element_type=F32)

    def tile(j, carry):
        m, acc, s = carry
        s_next = scores(j + 1)
        m, acc = absorb(j, s, m, acc)
        return m, acc, s_next

    init = (jnp.full((all_rows, 1), NEG, F32), jnp.zeros((all_rows, 2 * LANES), F32), scores(0))
    m, acc, s = lax.fori_loop(0, j_diag, tile, init)
    s = jnp.where(j_diag * kt + lane_kt <= rowpos, s, NEG)
    _, acc = absorb(j_diag, s, m, acc)
    o_sel = acc[:, 0:LANES] / acc[:, LANES:2 * LANES]
    ss, vs = [], []
    for j in range(n_win):
        start_true = q0 - WINDOW + j * LANES
        start = pl.multiple_of(jnp.maximum(start_true, 0), LANES)
        ss.append(_dot_nt(q_all, kvb_ref[0, pl.ds(start, LANES), KV_COLS:KV_COLS + LANES]))
        vs.append(kvb_ref[0, pl.ds(start, LANES), KV_COLS + LANES:2 * KV_COLS])
    kpos = q0 - WINDOW + lax.broadcasted_iota(jnp.int32, (1, n_win * LANES), 1)
    valid = (kpos <= rowpos) & (kpos > rowpos - WINDOW) & (kpos >= 0)
    s = jnp.where(valid, jnp.concatenate(ss, axis=1), NEG)
    e = jnp.where(valid, jnp.exp(s - jnp.max(s, axis=-1, keepdims=True)), 0.0).astype(BF16)
    acc = jnp.dot(e, with_ones(jnp.concatenate(vs, axis=0)), preferred_element_type=F32)
    o_win = acc[:, 0:LANES] / acc[:, LANES:2 * LANES]
    sel_pieces = [o_sel[g * rows:(g + 1) * rows] for g in range(NSA_KV_HEADS)]
    win_pieces = [o_win[g * rows:(g + 1) * rows] for g in range(NSA_KV_HEADS)]
    sel_cols = _assemble_heads(sel_pieces, tq)
    win_cols = _assemble_heads(win_pieces, tq)
    gate = gate_ref[...]
    for c in range(NSA_W // LANES):
        oc = oc_ref[:, c * LANES:(c + 1) * LANES]
        o = (_gate_col(gate, 0, c, tq) * oc + _gate_col(gate, 1, c, tq) * sel_cols[c]
             + _gate_col(gate, 2, c, tq) * win_cols[c])
        out_ref[:, c * LANES:(c + 1) * LANES] = o.astype(BF16)


def _sel_win(qpad, sel, gates, oc, kvb, *, batch, seq_len, tq):
    n = qpad.shape[0]
    nq = seq_len // tq
    ns2 = sel.shape[1]
    row = lambda w: pl.BlockSpec((tq, w), lambda b, i: (b * nq + i, 0))
    return pl.pallas_call(
        functools.partial(_sel_win_kernel, seq_len=seq_len),
        grid=(batch, nq),
        in_specs=[row(NSA_HEADS * LANES), row(ns2), row(GATE_PAD), row(NSA_W),
                  pl.BlockSpec((1, seq_len, 2 * KV_COLS), lambda b, i: (b, 0, 0))],
        out_specs=row(NSA_W),
        out_shape=jax.ShapeDtypeStruct((n, NSA_W), BF16),
        compiler_params=_cparams(("parallel", "parallel")),
        name="nsa_sel_win",
    )(qpad, sel, gates, oc, kvb.reshape(batch, seq_len, 2 * KV_COLS))


def _rwprep_kernel(zr_ref, prev_ref, halo_ref, mu_ref, vec_ref, w2_ref, a2_ref, g2_ref, rk_ref,
                   r_o, w_o, k_o, v_o, a_o, b_o, g_o, bonus_o, *, single, tiles_per_seq):
    tm = zr_ref.shape[0]
    zr = zr_ref[...]
    if single:
        prev = prev_ref[...]
    else:
        first_tile = pl.program_id(0) % tiles_per_seq == 0
        before = jnp.where(first_tile, prev_ref[0], halo_ref[7:8, :])
        rowi = lax.broadcasted_iota(jnp.int32, (tm, 1), 0)
        prev = jnp.where(rowi == 0, before, pltpu.roll(zr, 1, 0))
    zs = zr + (prev - zr) * mu_ref[...]
    r = zs[:, 0:RW_W]
    k = zs[:, RW_W:2 * RW_W]
    v = zs[:, 2 * RW_W:3 * RW_W]
    lora = zs[:, 3 * RW_W:3 * RW_W + W_LORA + A_LORA]
    w0, a0, k_k, k_a = (vec_ref[i:i + 1, :] for i in range(4))
    wlog = -jax.nn.softplus(-(w0 + _dot(jnp.tanh(lora), w2_ref[...]))) - 0.5
    log_decay = -jnp.exp(wlog)
    a = jax.nn.sigmoid(a0 + _dot(lora, a2_ref[...]))
    g = _dot(jax.nn.sigmoid(zs[:, 3 * RW_W + W_LORA + A_LORA:]), g2_ref[...])
    kk = k * k_k
    kk = kk / jnp.maximum(jnp.sqrt(_head_sum(kk * kk)), 1e-12)
    k2 = k * (1.0 + (a - 1.0) * k_a)
    r_o[...] = r
    w_o[...] = jnp.exp(log_decay) if single else log_decay
    k_o[...] = k2
    v_o[...] = v
    a_o[...] = -kk
    b_o[...] = kk * a
    g_o[...] = g
    bonus_o[...] = _head_sum(r * k2 * rk_ref[...]) * v


def _rw_prep(zr, shift_prev, P, *, seq_len, tm):
    n = zr.shape[0]
    single = seq_len == 1
    w2p = jnp.concatenate([P['rw_w2'], jnp.zeros((A_LORA, RW_W), F32)], axis=0).astype(BF16)
    a2p = jnp.concatenate([jnp.zeros((W_LORA, RW_W), F32), P['rw_a2']], axis=0).astype(BF16)
    const = lambda shape: pl.BlockSpec(shape, lambda i: (0,) * len(shape))
    if single:
        prev_arg = shift_prev
        prev_spec = pl.BlockSpec((tm, RW_COLS), lambda i: (i, 0))
        tiles = 1
    else:
        tiles = seq_len // tm
        prev_arg = shift_prev[:, None, :]
        prev_spec = pl.BlockSpec((1, 1, RW_COLS), lambda i: (i // tiles, 0, 0))
    halo_spec = pl.BlockSpec((8, RW_COLS), lambda i: (jnp.maximum(i * (tm // 8) - 1, 0), 0))
    row = pl.BlockSpec((tm, RW_W), lambda i: (i, 0))
    return pl.pallas_call(
        functools.partial(_rwprep_kernel, single=single, tiles_per_seq=tiles),
        grid=(n // tm,),
        in_specs=[pl.BlockSpec((tm, RW_COLS), lambda i: (i, 0)), prev_spec, halo_spec,
                  const((1, RW_COLS)), const((6, RW_W)), const((W_LORA + A_LORA, RW_W)),
                  const((W_LORA + A_LORA, RW_W)), const((G_LORA, RW_W)), const((1, RW_W))],
        out_specs=[row] * 8,
        out_shape=[jax.ShapeDtypeStruct((n, RW_W), F32)] * 8,
        compiler_params=_cparams(("parallel",)),
        name="rwkv_prep",
    )(zr, prev_arg, zr, P['rw_mu'][None, :], P['rw_vec'], w2p, a2p, P['rw_g2'].astype(BF16),
      P['rw_rk'].reshape(1, RW_W))


def _segsum(x, ones2):
    hi = x.astype(BF16)
    lo = (x - hi.astype(F32)).astype(BF16)
    return jnp.dot(jnp.concatenate([hi, lo], axis=1), ones2, preferred_element_type=F32)


def _scan_kernel(r_ref, w_ref, k_ref, v_ref, a_ref, b_ref, s0_ref, y_ref, sfin_ref, s_scr):
    nb, tc = r_ref.shape[0], r_ref.shape[1]
    npair = RW_HEADS // 2
    units = [(b, hp) for b in range(nb) for hp in range(npair)]

    @pl.when(pl.program_id(0) == 0)
    def _():
        s_scr[...] = s0_ref[...]

    ri = lax.broadcasted_iota(jnp.int32, (2 * LANES, LANES), 0) % LANES
    ci = lax.broadcasted_iota(jnp.int32, (2 * LANES, LANES), 1)
    ones2 = (ri // HEAD_DIM == ci // HEAD_DIM).astype(BF16)
    diag = (lax.broadcasted_iota(jnp.int32, (HEAD_DIM, LANES), 0)
            == lax.broadcasted_iota(jnp.int32, (HEAD_DIM, LANES), 1) % HEAD_DIM)

    nu = len(units)
    refs = dict(r=r_ref, w=w_ref, k=k_ref, v=v_ref, a=a_ref, b=b_ref)

    def run(base, nsteps):
        blocks = {name: [ref[b, pl.ds(base, nsteps), hp * LANES:(hp + 1) * LANES] for b, hp in units]
                  for name, ref in refs.items()}
        states = [s_scr[b, hp] for b, hp in units]
        yrows = [[] for _ in units]
        for t in range(nsteps):
            row = lambda name, i: blocks[name][i][t:t + 1, :]
            parts = [states[i] * row('a', i) for i in range(nu)]
            parts += [jnp.where(diag, row('v', i), 0.0) for i in range(nu)]
            red = _segsum(jnp.concatenate(parts, axis=0), ones2)
            for i in range(nu):
                sa = red[i * HEAD_DIM:(i + 1) * HEAD_DIM]
                vb = red[(nu + i) * HEAD_DIM:(nu + i + 1) * HEAD_DIM]
                states[i] = states[i] * row('w', i) + sa * row('b', i) + vb * row('k', i)
            yred = _segsum(jnp.concatenate([states[i] * row('r', i) for i in range(nu)], axis=0), ones2)
            for i in range(nu):
                yb = yred[i * HEAD_DIM:(i + 1) * HEAD_DIM]
                yrows[i].append(jnp.sum(jnp.where(diag, yb, 0.0), axis=0, keepdims=True))
        for i, (b, hp) in enumerate(units):
            s_scr[b, hp] = states[i]
            y_ref[b, pl.ds(base, nsteps), hp * LANES:(hp + 1) * LANES] = jnp.concatenate(yrows[i], axis=0)

    if tc % 8 == 0:
        def group(t8, carry):
            run(pl.multiple_of(t8 * 8, 8), 8)
            return carry

        lax.fori_loop(0, tc // 8, group, 0)
    else:
        run(0, tc)

    @pl.when(pl.program_id(0) == pl.num_programs(0) - 1)
    def _():
        sfin_ref[...] = s_scr[...]


def _state_to_pairs(s):
    nb = s.shape[0]
    return s.reshape(nb, 2, 2, HEAD_DIM, HEAD_DIM).transpose(0, 1, 3, 2, 4).reshape(nb, 2, HEAD_DIM, LANES)


def _pairs_to_state(s):
    nb = s.shape[0]
    return s.reshape(nb, 2, HEAD_DIM, 2, HEAD_DIM).transpose(0, 1, 3, 2, 4).reshape(nb, RW_HEADS, HEAD_DIM, HEAD_DIM)


def _rw_scan(r, w, k, v, a, b, s0, *, batch, seq_len, tc):
    args = [t.reshape(batch, seq_len, RW_W) for t in (r, w, k, v, a, b)]
    blk = pl.BlockSpec((batch, tc, RW_W), lambda c: (0, c, 0))
    st = pl.BlockSpec((batch, 2, HEAD_DIM, LANES), lambda c: (0, 0, 0, 0))
    y, sfin = pl.pallas_call(
        _scan_kernel,
        grid=(seq_len // tc,),
        in_specs=[blk] * 6 + [st],
        out_specs=[blk, st],
        out_shape=[jax.ShapeDtypeStruct((batch, seq_len, RW_W), F32),
                   jax.ShapeDtypeStruct((batch, 2, HEAD_DIM, LANES), F32)],
        scratch_shapes=[pltpu.VMEM((batch, 2, HEAD_DIM, LANES), F32)],
        compiler_params=_cparams(("arbitrary",)),
        name="rwkv_scan",
    )(*args, _state_to_pairs(s0))
    return y.reshape(batch * seq_len, RW_W), _pairs_to_state(sfin)


RW_CHUNK = 64


def _block_rows(x):
    c = x.shape[0]
    tiled = jnp.concatenate([x] * RW_HEADS, axis=0)
    same = (lax.broadcasted_iota(jnp.int32, tiled.shape, 0) // c
            == lax.broadcasted_iota(jnp.int32, tiled.shape, 1) // HEAD_DIM)
    return jnp.where(same, tiled, 0.0)


def _fold_rows(x_bd):
    c = x_bd.shape[0] // RW_HEADS
    return functools.reduce(lambda p, q: p + q, [x_bd[h * c:(h + 1) * c] for h in range(RW_HEADS)])


def _rw_chunk_prep_kernel(r_ref, lw_ref, k_ref, v_ref, a_ref, b_ref,
                          at_o, rt_o, tcat_o, brb_o, uv_o, yv_o, bc_o, kc_o, gend_o):
    c = RW_CHUNK
    n4 = RW_HEADS * c
    rowi = lax.broadcasted_iota(jnp.int32, (c, RW_W), 0)
    ti = lax.broadcasted_iota(jnp.int32, (n4, n4), 0)
    ji = lax.broadcasted_iota(jnp.int32, (n4, n4), 1)
    same_head = ti // c == ji // c
    strict = same_head & (ti % c > ji % c)
    incl = same_head & (ti % c >= ji % c)
    eye = (ti == ji).astype(F32)
    gends = []
    for ci in range(r_ref.shape[0] // c):
        rows = slice(ci * c, (ci + 1) * c)
        lw = lw_ref[rows, :]
        cs = lw
        shift = 1
        while shift < c:
            cs = cs + jnp.where(rowi >= shift, pltpu.roll(cs, shift, 0), 0.0)
            shift *= 2
        cs_end = cs[c - 1:c, :]
        gends.append(jnp.exp(cs_end))
        inv = jnp.exp(-cs)
        to_end = jnp.exp(cs_end - cs)
        at = a_ref[rows, :] * jnp.exp(cs - lw)
        rt = r_ref[rows, :] * jnp.exp(cs)
        b, k, v = b_ref[rows, :], k_ref[rows, :], v_ref[rows, :]
        bh4 = jnp.concatenate([b * inv] * RW_HEADS, axis=0)
        kh4 = jnp.concatenate([k * inv] * RW_HEADS, axis=0)
        at_bd, rt_bd, v_bd = _block_rows(at), _block_rows(rt), _block_rows(v)
        a_ab = jnp.where(strict, _dot_nt(at_bd, bh4), 0.0)
        a_ak = jnp.where(strict, _dot_nt(at_bd, kh4), 0.0)
        b_rb = jnp.where(incl, _dot_nt(rt_bd, bh4), 0.0)
        b_rk = jnp.where(incl, _dot_nt(rt_bd, kh4), 0.0)
        pair = same_head & ((ti % c) // 2 == (ji % c) // 2)
        t_inv = eye + jnp.where(pair, a_ab, 0.0)
        s = 2
        while s < c:
            lower_left = (((ti % c) // (2 * s) == (ji % c) // (2 * s))
                          & (((ti % c) // s) % 2 == 1) & (((ji % c) // s) % 2 == 0))
            t_inv = t_inv + _dot(_dot(t_inv, jnp.where(lower_left, a_ab, 0.0)), t_inv)
            s *= 2
        at_o[rows, :] = at.astype(BF16)
        rt_o[rows, :] = rt.astype(BF16)
        tcat_o[rows, :] = _fold_rows(t_inv).astype(BF16)
        brb_o[rows, :] = _fold_rows(b_rb).astype(BF16)
        uv_o[rows, :] = _fold_rows(_dot(t_inv, _dot(a_ak, v_bd)))
        yv_o[rows, :] = _fold_rows(_dot(b_rk, v_bd))
        bc_o[rows, :] = (b * to_end).astype(BF16)
        kc_o[rows, :] = (k * to_end).astype(BF16)
    gend_o[...] = jnp.concatenate(gends, axis=0)


def _rw_chain_kernel(at_ref, rt_ref, tcat_ref, brb_ref, uv_ref, yv_ref, bc_ref, kc_ref, v_ref, gend_ref,
                     y_ref, sfin_ref, s_scr):
    nb, tt = at_ref.shape[0], at_ref.shape[1]
    c = RW_CHUNK

    @pl.when(pl.program_id(0) == 0)
    def _():
        s_scr[...] = jnp.zeros_like(s_scr)

    same_head = (lax.broadcasted_iota(jnp.int32, (RW_W, RW_W), 0) // HEAD_DIM
                 == lax.broadcasted_iota(jnp.int32, (RW_W, RW_W), 1) // HEAD_DIM)
    for ci in range(tt // c):
        rows = slice(ci * c, (ci + 1) * c)
        for b in range(nb):
            state = s_scr[b]
            m1 = _dot_nt(jnp.concatenate([at_ref[b, rows, :], rt_ref[b, rows, :]], axis=0), state)
            u = _dot(tcat_ref[b, rows, :], _block_rows(m1[0:c])) + uv_ref[b, rows, :]
            y_ref[b, rows, :] = m1[c:2 * c] + _dot(brb_ref[b, rows, :], _block_rows(u)) + yv_ref[b, rows, :]
            grow = lax.dot_general(
                jnp.concatenate([u, v_ref[b, rows, :]], axis=0).astype(BF16),
                jnp.concatenate([bc_ref[b, rows, :], kc_ref[b, rows, :]], axis=0),
                (((0,), (0,)), ((), ())), preferred_element_type=F32)
            s_scr[b] = state * gend_ref[b, ci:ci + 1, :] + jnp.where(same_head, grow, 0.0)

    @pl.when(pl.program_id(0) == pl.num_programs(0) - 1)
    def _():
        for b in range(nb):
            for h in range(RW_HEADS):
                sfin_ref[b, h] = s_scr[b, h * HEAD_DIM:(h + 1) * HEAD_DIM, h * HEAD_DIM:(h + 1) * HEAD_DIM]


def _rw_chunked(r, lw, k, v, a, b, *, batch, seq_len, tt):
    n = r.shape[0]
    per_tile = tt // RW_CHUNK
    row = pl.BlockSpec((tt, RW_W), lambda i: (i, 0))
    outs = [BF16, BF16, BF16, BF16, F32, F32, BF16, BF16]
    *prep, gend = pl.pallas_call(
        _rw_chunk_prep_kernel,
        grid=(n // tt,),
        in_specs=[row] * 6,
        out_specs=[row] * 8 + [pl.BlockSpec((per_tile, RW_W), lambda i: (i, 0))],
        out_shape=[jax.ShapeDtypeStruct((n, RW_W), dt) for dt in outs]
        + [jax.ShapeDtypeStruct((n // RW_CHUNK, RW_W), F32)],
        compiler_params=_cparams(("parallel",)),
        name="rwkv_chunk_prep",
    )(r, lw, k, v, a, b)
    seq = lambda t: t.reshape(batch, seq_len, RW_W)
    blk = pl.BlockSpec((batch, tt, RW_W), lambda i: (0, i, 0))
    y, s_fin = pl.pallas_call(
        _rw_chain_kernel,
        grid=(seq_len // tt,),
        in_specs=[blk] * 9 + [pl.BlockSpec((batch, per_tile, RW_W), lambda i: (0, i, 0))],
        out_specs=[blk, pl.BlockSpec((batch, RW_HEADS, HEAD_DIM, HEAD_DIM), lambda i: (0, 0, 0, 0))],
        out_shape=[jax.ShapeDtypeStruct((batch, seq_len, RW_W), F32),
                   jax.ShapeDtypeStruct((batch, RW_HEADS, HEAD_DIM, HEAD_DIM), F32)],
        scratch_shapes=[pltpu.VMEM((batch, RW_W, RW_W), F32)],
        compiler_params=_cparams(("arbitrary",)),
        name="rwkv_chain",
    )(*[seq(t) for t in prep], seq(v), gend.reshape(batch, seq_len // RW_CHUNK, RW_W))
    return y.reshape(n, RW_W), s_fin


def _mid_kernel(x_ref, ogm_ref, onsa_ref, y_ref, g_ref, bonus_ref, ng_ref, ln_ref, wo_ref,
                wq_ref, kv_ref, wmo_ref, out_ref):
    tm = x_ref.shape[-2]
    rows = max(tm, 8)
    ld = lambda ref: jnp.broadcast_to(ref[...].reshape(tm, ref.shape[-1]), (rows, ref.shape[-1]))
    x = ld(x_ref)
    y = ld(y_ref)
    ym = _head_sum(y) * (1.0 / HEAD_DIM)
    yc = y - ym
    yv = _head_sum(yc * yc) * (1.0 / HEAD_DIM)
    yn = yc * lax.rsqrt(yv + RW_LN_EPS) * ln_ref[0:1, :] + ln_ref[1:2, :]
    o_rw = (yn + ld(bonus_ref)) * ld(g_ref)
    mix = (_dot(ld(ogm_ref), wo_ref[0:GM_W, :]) + _dot(ld(onsa_ref), wo_ref[GM_W:GM_W + NSA_W, :])
           + _dot(o_rw, wo_ref[GM_W + NSA_W:, :]))
    x = x + _rms(mix, ng_ref[1:2, :])
    q = _dot(_rms(x, ng_ref[2:3, :]), wq_ref[...]) * (MEM_HEAD_DIM ** -0.5)
    kv = kv_ref[0]
    heads = []
    for hh in range(MEM_HEADS):
        lo, hi = hh * MEM_HEAD_DIM, (hh + 1) * MEM_HEAD_DIM
        s = _dot_nt(q[:, lo:hi], kv[:, lo:hi])
        e = jnp.exp(s - jnp.max(s, axis=-1, keepdims=True))
        p = e / jnp.sum(e, axis=-1, keepdims=True)
        heads.append(_dot(p, kv[:, D_MODEL + lo:D_MODEL + hi]))
    o = _dot(jnp.concatenate(heads, axis=1), wmo_ref[...])
    x = x + _rms(o, ng_ref[3:4, :])
    out_ref[...] = x[0:tm].reshape(out_ref.shape)


def _mid(x, ogm, onsa, y, g, bonus, kv_mem, P, *, batch, seq_len, tm):
    n = x.shape[0]
    tiles = seq_len // tm
    if tm >= 8:
        row = lambda w: pl.BlockSpec((tm, w), lambda i: (i, 0))
        shp = lambda t: t
        out_shape = jax.ShapeDtypeStruct((n, D_MODEL), F32)
    else:
        row = lambda w: pl.BlockSpec((1, tm, w), lambda i: (i, 0, 0))
        shp = lambda t: t.reshape(n // tm, tm, t.shape[-1])
        out_shape = jax.ShapeDtypeStruct((n // tm, tm, D_MODEL), F32)
    const = lambda shape: pl.BlockSpec(shape, lambda i: (0,) * len(shape))
    out = pl.pallas_call(
        _mid_kernel,
        grid=(n // tm,),
        in_specs=[row(D_MODEL), row(GM_W), row(NSA_W), row(RW_W), row(RW_W), row(RW_W),
                  const((6, D_MODEL)), const((2, RW_W)), const((D_MODEL, D_MODEL)),
                  const((D_MODEL, D_MODEL)),
                  pl.BlockSpec((1, N_MEM, 2 * D_MODEL), lambda i: (i // tiles, 0, 0)),
                  const((D_MODEL, D_MODEL))],
        out_specs=row(D_MODEL),
        out_shape=out_shape,
        compiler_params=_cparams(("parallel",)),
        name="mix_out_mem_attn",
    )(shp(x), shp(ogm), shp(onsa), shp(y), shp(g), shp(bonus), P['norm_g'], P['rw_vec'][4:6],
      P['w_out'].astype(BF16), P['w_mem_q'].astype(BF16), kv_mem, P['w_mem_o'].astype(BF16))
    return out.reshape(n, D_MODEL)


def _memkv_kernel(x_ref, g_ref, w_ref, o_ref, ob_ref):
    o = _dot(_rms(x_ref[...], g_ref[...]), w_ref[...])
    o_ref[...] = o
    ob_ref[...] = o.astype(BF16)


def _mem_kv(mem, g, w_kv, *, tm):
    n = mem.shape[0]
    return pl.pallas_call(
        _memkv_kernel,
        grid=(n // tm,),
        in_specs=[pl.BlockSpec((tm, D_MODEL), lambda i: (i, 0)), pl.BlockSpec((1, D_MODEL), lambda i: (0, 0)),
                  pl.BlockSpec((D_MODEL, 2 * D_MODEL), lambda i: (0, 0))],
        out_specs=[pl.BlockSpec((tm, 2 * D_MODEL), lambda i: (i, 0))] * 2,
        out_shape=[jax.ShapeDtypeStruct((n, 2 * D_MODEL), F32), jax.ShapeDtypeStruct((n, 2 * D_MODEL), BF16)],
        compiler_params=_cparams(("parallel",)),
        name="mem_kv_proj",
    )(mem, g[None, :], w_kv.astype(BF16))


def _ffn_kernel(x_ref, prev_ref, prev1_ref, ng_ref, wg_ref, wu_ref, cw_ref, cb_ref, wo_ref,
                out_ref, tail_ref, h_scr, acc_scr, tail_scr, *, single, tiles_per_seq):
    tm = x_ref.shape[0]
    i, j = pl.program_id(0), pl.program_id(1)

    @pl.when(j == 0)
    def _():
        h_scr[...] = _rms(x_ref[...], ng_ref[4:5, :]).astype(BF16)
        acc_scr[...] = jnp.zeros_like(acc_scr)

    h = h_scr[...]
    gate = jnp.dot(h, wg_ref[...], preferred_element_type=F32)
    up = jnp.dot(h, wu_ref[...], preferred_element_type=F32)
    if single:
        g2 = prev_ref[...]
        g1 = prev1_ref[...]
        tail_ref[...] = gate
    else:
        first = i % tiles_per_seq == 0
        t0 = jnp.where(first, prev_ref[0, 0:1, :], tail_scr[j, 6:7, :])
        t1 = jnp.where(first, prev_ref[0, 1:2, :], tail_scr[j, 7:8, :])
        rowi = lax.broadcasted_iota(jnp.int32, (tm, 1), 0)
        g1 = jnp.where(rowi == 0, t1, pltpu.roll(gate, 1, 0))
        g2 = jnp.where(rowi == 0, t0, jnp.where(rowi == 1, t1, pltpu.roll(gate, 2, 0)))
        tail_scr[j] = gate[tm - 8:tm]
        tail_ref[0] = gate[tm - 2:tm]
    conv = cb_ref[...] + g2 * cw_ref[0:1, :] + g1 * cw_ref[1:2, :] + gate * cw_ref[2:3, :]
    act = jax.nn.silu(conv) * up
    acc_scr[...] += _dot(act, wo_ref[...])

    @pl.when(j == pl.num_programs(1) - 1)
    def _():
        out_ref[...] = x_ref[...] + _rms(acc_scr[...], ng_ref[5:6, :])


def _ffn(x, conv_prev, P, *, seq_len, tm, tf):
    n = x.shape[0]
    single = seq_len == 1
    nj = D_FF // tf
    w_in = P['ffn_w_in'].astype(BF16)
    if single:
        prev_args = (conv_prev[:, 0], conv_prev[:, 1])
        prev_specs = [pl.BlockSpec((tm, tf), lambda i, j: (i, j))] * 2
        tail_spec = pl.BlockSpec((tm, tf), lambda i, j: (i, j))
        tail_shape = jax.ShapeDtypeStruct((n, D_FF), F32)
        tiles = 1
    else:
        tiles = seq_len // tm
        prev_args = (conv_prev, conv_prev)
        prev_specs = [pl.BlockSpec((1, CONV_W - 1, tf), lambda i, j: (i // tiles, 0, j))] * 2
        tail_spec = pl.BlockSpec((1, CONV_W - 1, tf), lambda i, j: (i, 0, j))
        tail_shape = jax.ShapeDtypeStruct((n // tm, CONV_W - 1, D_FF), F32)
    out, tail = pl.pallas_call(
        functools.partial(_ffn_kernel, single=single, tiles_per_seq=tiles),
        grid=(n // tm, nj),
        in_specs=[pl.BlockSpec((tm, D_MODEL), lambda i, j: (i, 0)), *prev_specs,
                  pl.BlockSpec((6, D_MODEL), lambda i, j: (0, 0)),
                  pl.BlockSpec((D_MODEL, tf), lambda i, j: (0, j)),
                  pl.BlockSpec((D_MODEL, tf), lambda i, j: (0, nj + j)),
                  pl.BlockSpec((CONV_W, tf), lambda i, j: (0, j)),
                  pl.BlockSpec((1, tf), lambda i, j: (0, j)),
                  pl.BlockSpec((tf, D_MODEL), lambda i, j: (j, 0))],
        out_specs=[pl.BlockSpec((tm, D_MODEL), lambda i, j: (i, 0)), tail_spec],
        out_shape=[jax.ShapeDtypeStruct((n, D_MODEL), F32), tail_shape],
        scratch_shapes=[pltpu.VMEM((tm, D_MODEL), BF16), pltpu.VMEM((tm, D_MODEL), F32),
                        pltpu.VMEM((nj, 8, tf), F32)],
        compiler_params=_cparams(("arbitrary", "arbitrary")),
        name="conv_ffn",
    )(x, *prev_args, P['norm_g'], w_in, w_in, P['ffn_conv_w'], P['ffn_conv_b'][None, :],
      P['ffn_w_out'].astype(BF16))
    if single:
        tail = jnp.stack([conv_prev[:, 1], tail], axis=1)
    else:
        tail = tail[tiles - 1::tiles]
    return out, tail


def _sample_cmp_win_kernel(pt_ref, qpad_ref, kvc_hbm, win_ref, kvw_ref, oc_ref, ow_ref, idx_ref,
                           kbuf, sem):
    b = pl.program_id(0)
    n_pages = kbuf.shape[0]
    per_page = PAGE_SIZE // CMP_BLOCK

    def page_copy(p):
        return pltpu.make_async_copy(kvc_hbm.at[pl.ds(pt_ref[b, p], 1)], kbuf.at[pl.ds(p, 1)], sem)

    def start(p, c):
        page_copy(p).start()
        return c

    def wait(p, c):
        page_copy(p).wait()
        return c

    lax.fori_loop(0, n_pages, start, 0)
    lax.fori_loop(0, n_pages, wait, 0)
    q8 = jnp.concatenate([qpad_ref[0, :, h * LANES:(h + 1) * LANES].astype(F32) for h in range(NSA_HEADS)],
                         axis=0)
    ss = [_dot_nt(q8, kbuf[:, j * KV_COLS:j * KV_COLS + LANES]) for j in range(per_page)]
    mx = functools.reduce(jnp.maximum, [jnp.max(s, axis=-1, keepdims=True) for s in ss])
    es = [jnp.exp(s - mx) for s in ss]
    den = functools.reduce(lambda x, y: x + y, [jnp.sum(e, axis=-1, keepdims=True) for e in es])
    ps = [e / den for e in es]
    oc = functools.reduce(lambda x, y: x + y,
                          [_dot(ps[j], kbuf[:, j * KV_COLS + LANES:(j + 1) * KV_COLS]) for j in range(per_page)])
    group0 = lax.broadcasted_iota(jnp.int32, (NSA_HEADS, HEAD_DIM), 0) < NSA_REP
    own_half = lambda o: jnp.where(group0, o[:, 0:HEAD_DIM], o[:, HEAD_DIM:LANES])
    oc_ref[0] = own_half(oc)
    lane = lax.broadcasted_iota(jnp.int32, (NSA_KV_HEADS, n_pages), 1)
    ns_past = 2 * n_pages
    vals = []
    for jj in range(2):
        pj = ps[2 * jj] + ps[2 * jj + 1]
        rows = [jnp.sum(pj[g * NSA_REP:(g + 1) * NSA_REP], axis=0, keepdims=True) for g in range(NSA_KV_HEADS)]
        val = jnp.concatenate(rows, axis=0)
        n_of = 2 * lane + jj
        vals.append(jnp.where(n_of == 0, FORCE, val))
    cur_val = jnp.full((NSA_KV_HEADS, 1), FORCE, F32)
    big = ns_past + 1
    out_lane = lax.broadcasted_iota(jnp.int32, (NSA_KV_HEADS, LANES), 1)
    idx = jnp.zeros((NSA_KV_HEADS, LANES), jnp.int32)
    for rnd in range(TOP_N):
        mx = jnp.maximum(jnp.maximum(jnp.max(vals[0], axis=-1, keepdims=True),
                                     jnp.max(vals[1], axis=-1, keepdims=True)), cur_val)
        c0 = jnp.min(jnp.where(vals[0] == mx, 2 * lane, big), axis=-1, keepdims=True)
        c1 = jnp.min(jnp.where(vals[1] == mx, 2 * lane + 1, big), axis=-1, keepdims=True)
        c2 = jnp.where(cur_val == mx, ns_past, big)
        pick = jnp.minimum(jnp.minimum(c0, c1), c2)
        idx = jnp.where(out_lane == rnd, pick, idx)
        vals[0] = jnp.where(2 * lane == pick, -jnp.inf, vals[0])
        vals[1] = jnp.where(2 * lane + 1 == pick, -jnp.inf, vals[1])
        cur_val = jnp.where(pick == ns_past, -jnp.inf, cur_val)
    idx_ref[0] = jnp.concatenate([idx, jnp.zeros((8 - NSA_KV_HEADS, LANES), jnp.int32)], axis=0)
    wb = win_ref.shape[1]
    s_old = _dot_nt(q8, win_ref[0, :, 0:LANES])
    new_k = jnp.broadcast_to(kvw_ref[0, :, 0:LANES], (8, LANES))
    s_new = _dot_nt(q8, new_k)[:, 0:1]
    keep = lax.broadcasted_iota(jnp.int32, (1, wb), 1) > wb - WINDOW
    s_old = jnp.where(keep, s_old, NEG)
    mx = jnp.maximum(jnp.max(s_old, axis=-1, keepdims=True), s_new)
    e_old = jnp.exp(s_old - mx)
    e_new = jnp.exp(s_new - mx)
    den = jnp.sum(e_old, axis=-1, keepdims=True) + e_new
    new_v = kvw_ref[0, :, LANES:KV_COLS].astype(BF16).astype(F32)
    ow_ref[0] = own_half(_dot(e_old / den, win_ref[0, :, LANES:KV_COLS])
                         + (e_new / den).astype(BF16).astype(F32) * new_v)


def _sample_cmp_win(page_table, qpad, kvc_phys, win_buf, kvw):
    nb, n_pages = page_table.shape
    wb = win_buf.shape[1]
    row_w = (PAGE_SIZE // CMP_BLOCK) * KV_COLS
    piece = jax.ShapeDtypeStruct((nb, NSA_HEADS, HEAD_DIM), F32)
    grid_spec = pltpu.PrefetchScalarGridSpec(
        num_scalar_prefetch=1,
        grid=(nb,),
        in_specs=[pl.BlockSpec((1, 1, NSA_HEADS * LANES), lambda b, pt: (b, 0, 0)),
                  pl.BlockSpec(memory_space=pl.ANY),
                  pl.BlockSpec((1, wb, KV_COLS), lambda b, pt: (b, 0, 0)),
                  pl.BlockSpec((1, 1, KV_COLS), lambda b, pt: (b, 0, 0))],
        out_specs=[pl.BlockSpec((1, NSA_HEADS, HEAD_DIM), lambda b, pt: (b, 0, 0))] * 2
        + [pl.BlockSpec((1, 8, LANES), lambda b, pt: (b, 0, 0))],
        scratch_shapes=[pltpu.VMEM((n_pages, row_w), F32), pltpu.SemaphoreType.DMA(())],
    )
    return pl.pallas_call(
        _sample_cmp_win_kernel,
        grid_spec=grid_spec,
        out_shape=[piece, piece, jax.ShapeDtypeStruct((nb, 8, LANES), jnp.int32)],
        compiler_params=_cparams(("arbitrary",)),
        name="nsa_sample_cmp_win",
    )(page_table, qpad.reshape(nb, 1, NSA_HEADS * LANES), kvc_phys.reshape(-1, row_w), win_buf,
      kvw.reshape(nb, 1, KV_COLS))


def _sample_sel_kernel(pt_ref, idx_ref, q_ref, *refs, n_pages):
    blk_refs = refs[:TOP_N]
    kvs_ref, gate_ref, oc_ref, ow_ref, out_ref = refs[TOP_N:]
    b, g = pl.program_id(0), pl.program_id(1)
    q8 = jnp.concatenate([q_ref[0, 0], jnp.zeros((8 - NSA_REP, HEAD_DIM), F32)], axis=0)
    kv_new = kvs_ref[0]
    k_new = jnp.where(g == 0, kv_new[0:1], kv_new[1:2])
    v_new = jnp.where(g == 0, kv_new[2:3], kv_new[3:4])
    s_new = _dot_nt(q8, jnp.broadcast_to(k_new, (8, HEAD_DIM)))[:, 0:1]
    lane = lax.broadcasted_iota(jnp.int32, (8, PAGE_SIZE), 1)
    is_cur, ss = [], []
    for n in range(TOP_N):
        blk = idx_ref[b, g * TOP_N + n]
        is_cur.append(blk >= 2 * n_pages)
        half = blk % (PAGE_SIZE // SLC_BLOCK)
        s_page = _dot(q8, blk_refs[n][0, 0, 0, 0])
        ss.append(jnp.where(is_cur[n], jnp.where(lane == 0, s_new, NEG),
                            jnp.where(lane // SLC_BLOCK == half, s_page, NEG)))
    m = functools.reduce(jnp.maximum, [jnp.max(s, axis=-1, keepdims=True) for s in ss])
    ps = [jnp.exp(s - m) for s in ss]
    den = functools.reduce(lambda x, y: x + y, [jnp.sum(p, axis=-1, keepdims=True) for p in ps])
    v_new_r = v_new.astype(BF16).astype(F32)
    acc = jnp.zeros((8, HEAD_DIM), F32)
    for n in range(TOP_N):
        pv_new = ps[n][:, 0:1].astype(BF16).astype(F32) * v_new_r
        acc = acc + jnp.where(is_cur[n], pv_new, _dot_nt(ps[n], blk_refs[n][0, 0, 1, 0]))
    o_s = (acc / den)[0:NSA_REP]
    head = g * NSA_REP + lax.broadcasted_iota(jnp.int32, (NSA_REP, GATE_PAD), 0)
    lane_g = lax.broadcasted_iota(jnp.int32, (NSA_REP, GATE_PAD), 1)
    gate = jnp.broadcast_to(gate_ref[0], (NSA_REP, GATE_PAD))
    gcol = lambda br: jnp.sum(jnp.where(lane_g == br * NSA_HEADS + head, gate, 0.0), axis=-1, keepdims=True)
    out_ref[0, 0] = gcol(0) * oc_ref[0, 0] + gcol(1) * o_s + gcol(2) * ow_ref[0, 0]


def _sample_sel(page_table, idx, q, pool_pages, layer, kvs, gates, oc, ow):
    nb, n_pages = page_table.shape
    halves = PAGE_SIZE // SLC_BLOCK

    def page_spec(n):
        def blk_map(b, g, pt, ix):
            page = jnp.minimum(ix[b, g * TOP_N + n] // halves, n_pages - 1)
            return (layer, pt[b, page], 0, g, 0, 0)
        return pl.BlockSpec((1, 1, 2, 1, HEAD_DIM, PAGE_SIZE), blk_map)

    per_bg = pl.BlockSpec((1, 1, NSA_REP, HEAD_DIM), lambda b, g, pt, ix: (b, g, 0, 0))
    per_b = lambda shape: pl.BlockSpec((1,) + shape, lambda b, g, pt, ix: (b, 0, 0))
    grid_spec = pltpu.PrefetchScalarGridSpec(
        num_scalar_prefetch=2,
        grid=(nb, NSA_KV_HEADS),
        in_specs=[per_bg] + [page_spec(n) for n in range(TOP_N)]
        + [per_b((2 * NSA_KV_HEADS, HEAD_DIM)), per_b((1, GATE_PAD)), per_bg, per_bg],
        out_specs=per_bg,
    )
    grouped = lambda t: t.reshape(nb, NSA_KV_HEADS, NSA_REP, HEAD_DIM)
    out = pl.pallas_call(
        functools.partial(_sample_sel_kernel, n_pages=n_pages),
        grid_spec=grid_spec,
        out_shape=jax.ShapeDtypeStruct((nb, NSA_KV_HEADS, NSA_REP, HEAD_DIM), F32),
        compiler_params=_cparams(("arbitrary", "arbitrary")),
        name="nsa_sample_sel",
    )(page_table, idx, grouped(q), *([pool_pages] * TOP_N), kvs.reshape(nb, 2 * NSA_KV_HEADS, HEAD_DIM),
      gates.reshape(nb, 1, GATE_PAD), grouped(oc), grouped(ow))
    return out.reshape(nb, NSA_W)


def _tile(n, want):
    t = min(n, want)
    while n % t or (t % 8 and t != n):
        t -= 1
    return t


def _prompt_layer(x, tabs, mem_prompt, P, *, batch, seq_len):
    kv_f32, kv_b16 = _mem_kv(mem_prompt, P['mem_g'], P['w_mem_kv'], tm=_tile(mem_prompt.shape[0], 256))
    w_pack = _pack_w_in(P['w_in'])
    ogm, _, qpad, kvc, kvs, kvw, kvb, gates, zr = _in_proj(
        x, P['norm_g'][0], w_pack, tabs[0], tabs[1], P['gm_ln'], P['gm_ws'], P['gm_bs'],
        seq_len=seq_len, tm=_tile(seq_len, 256))
    pe, wb = _compress_weights(P['cmp_pe'], P['cmp_w'])
    nc = seq_len // CMP_BLOCK
    comp = _compress(kvc, pe, wb, tr=seq_len)
    comp = comp.reshape(batch, nc // 2, 2, KV_COLS).transpose(0, 2, 1, 3).reshape(batch, nc, KV_COLS)
    oc, sel = _cmp_attn(qpad, comp, batch=batch, seq_len=seq_len, tq=_tile(seq_len, 256))
    onsa = _sel_win(qpad, sel, gates, oc, kvb, batch=batch, seq_len=seq_len, tq=128)
    shift0 = jnp.zeros((batch, RW_COLS), F32)
    r, lw, k, v, a, b, g, bonus = _rw_prep(zr, shift0, P, seq_len=seq_len, tm=_tile(seq_len, 512))
    y, s_fin = _rw_chunked(r, lw, k, v, a, b, batch=batch, seq_len=seq_len, tt=_tile(seq_len, 512))
    x = _mid(x, ogm, onsa, y, g, bonus, kv_b16.reshape(batch, N_MEM, 2 * D_MODEL), P,
             batch=batch, seq_len=seq_len, tm=_tile(seq_len, 256))
    conv0 = jnp.zeros((batch, CONV_W - 1, D_FF), F32)
    x, conv_new = _ffn(x, conv0, P, seq_len=seq_len, tm=_tile(seq_len, 512), tf=D_FF // 2)
    kvshape = (batch, seq_len, 2, NSA_KV_HEADS, HEAD_DIM)
    wbp = min(WINDOW, seq_len)
    states = dict(
        cmp=kvc.reshape(kvshape), slc=kvs.reshape(kvshape), win=kvw.reshape(kvshape)[:, seq_len - wbp:],
        rw=s_fin, shift=zr.reshape(batch, seq_len, RW_COLS)[:, -1], conv=conv_new,
        mem=kv_f32.reshape(batch, N_MEM, 2, MEM_HEADS, MEM_HEAD_DIM))
    return x, states


def _sample_layer(x, tabs, page_table, cmp_pages, slc_pages, layer, win_buf, mem_kv, rw_state, shift_prev,
                  conv_prev, P):
    nb = x.shape[0]
    w_pack = _pack_w_in(P['w_in'])
    ogm, vgm, qpad, kvc, kvs, kvw, _, gates, zr, q = _in_proj(
        x, P['norm_g'][0], w_pack, tabs[0], tabs[1], P['gm_ln'], P['gm_ws'], P['gm_bs'], seq_len=1, tm=nb)
    pe, wb = _compress_weights(P['cmp_pe'], P['cmp_w'])
    n_phys = cmp_pages.shape[1]
    kvc_phys = _compress_pages(cmp_pages, layer, pe, wb, pages=_tile(n_phys, 64))
    oc, ow, idx = _sample_cmp_win(page_table, qpad, kvc_phys, win_buf.reshape(nb, -1, KV_COLS), kvw)
    idx = idx[:, :NSA_KV_HEADS, :TOP_N].reshape(nb, NSA_KV_HEADS * TOP_N)
    slc6 = slc_pages.reshape(slc_pages.shape[0], n_phys, 2, NSA_KV_HEADS, HEAD_DIM, PAGE_SIZE)
    onsa = _sample_sel(page_table, idx, q, slc6, layer, kvs, gates, oc, ow)
    r, w, k, v, a, b, g, bonus = _rw_prep(zr, shift_prev, P, seq_len=1, tm=nb)
    y, s_fin = _rw_scan(r, w, k, v, a, b, rw_state, batch=nb, seq_len=1, tc=1)
    x = _mid(x, ogm, onsa, y, g, bonus, mem_kv.reshape(nb, N_MEM, 2 * D_MODEL), P, batch=nb, seq_len=1, tm=1)
    x, conv_new = _ffn(x, conv_prev, P, seq_len=1, tm=nb, tf=256)
    kvshape = (nb, 1, 2, NSA_KV_HEADS, HEAD_DIM)
    states = dict(cmp=kvc.reshape(kvshape), slc=kvs.reshape(kvshape), win=kvw.reshape(kvshape), rw=s_fin,
                  shift=zr, conv=conv_new, gv=vgm.reshape(nb, 1, GM_W))
    return x, states


def kernel(x_prompt, x_sample, cache_cmp_kv, cache_slc_kv, cache_win_kv, cache_mem_kv, state_rwkv, state_rwkv_shift, state_ffn_conv, page_table, mem_prompt, norm_g, w_in, w_out, gm_ln, gm_ws, gm_bs, cmp_pe, cmp_w, rw_mu, rw_vec, rw_w2, rw_a2, rw_g2, rw_rk, mem_g, w_mem_q, w_mem_kv, w_mem_o, ffn_w_in, ffn_conv_w, ffn_conv_b, ffn_w_out):
    bp, tp = x_prompt.shape[:2]
    nb, ts = x_sample.shape[:2]
    assert ts == 1 and tp % 512 == 0
    depth = norm_g.shape[0]
    past = page_table.shape[1] * PAGE_SIZE
    tabs_p = _rope_tables(jnp.arange(tp))
    tabs_s = _rope_tables(jnp.full((nb,), past))
    xp = x_prompt.reshape(bp * tp, D_MODEL)
    xs = x_sample.reshape(nb, D_MODEL)
    mem = mem_prompt.reshape(bp * N_MEM, D_MODEL)
    weights = dict(norm_g=norm_g, w_in=w_in, w_out=w_out, gm_ln=gm_ln, gm_ws=gm_ws, gm_bs=gm_bs,
                   cmp_pe=cmp_pe, cmp_w=cmp_w, rw_mu=rw_mu, rw_vec=rw_vec, rw_w2=rw_w2, rw_a2=rw_a2,
                   rw_g2=rw_g2, rw_rk=rw_rk, mem_g=mem_g, w_mem_q=w_mem_q, w_mem_kv=w_mem_kv,
                   w_mem_o=w_mem_o, ffn_w_in=ffn_w_in, ffn_conv_w=ffn_conv_w, ffn_conv_b=ffn_conv_b,
                   ffn_w_out=ffn_w_out)
    cmp_pages = _pool_pages(cache_cmp_kv)
    slc_pages = _pool_pages(cache_slc_kv)
    ps, ss = [], []
    for l in range(depth):
        P = {name: val[l] for name, val in weights.items()}
        xp, st = _prompt_layer(xp, tabs_p, mem, P, batch=bp, seq_len=tp)
        ps.append(st)
        xs, st = _sample_layer(xs, tabs_s, page_table, cmp_pages, slc_pages, l, cache_win_kv[l],
                               cache_mem_kv[l], state_rwkv[l], state_rwkv_shift[l], state_ffn_conv[l], P)
        ss.append(st)
    stack = lambda sts, name: jnp.stack([st[name] for st in sts])
    return (xp.reshape(bp, tp, D_MODEL), xs.reshape(nb, 1, D_MODEL),
            stack(ps, 'cmp'), stack(ps, 'slc'), stack(ps, 'win'), stack(ps, 'rw'), stack(ps, 'shift'),
            stack(ps, 'conv'), stack(ps, 'mem'),
            stack(ss, 'cmp'), stack(ss, 'slc'), stack(ss, 'win'), stack(ss, 'rw'), stack(ss, 'shift'),
            stack(ss, 'conv'), stack(ss, 'gv'))
```

```python
import functools
import math

import jax
import jax.numpy as jnp
from jax import lax
from jax.experimental import pallas as pl
from jax.experimental.pallas import tpu as pltpu

F32 = jnp.float32
BF16 = jnp.bfloat16

D_MODEL = 1024
HEAD_DIM = 64
GM_W = 256
GM_HEADS = 4
CHUNK = 128
NSA_W = 512
NSA_HEADS = 8
NSA_KV_HEADS = 2
NSA_REP = 4
CMP_BLOCK = 32
SLC_BLOCK = 64
TOP_N = 16
WINDOW = 512
RW_W = 256
RW_HEADS = 4
W_LORA = 64
A_LORA = 64
G_LORA = 128
RW_COLS = 3 * RW_W + W_LORA + A_LORA + G_LORA
RW_LN_EPS = 64e-5
N_MEM = 256
MEM_HEADS = 4
MEM_HEAD_DIM = 256
D_FF = 2816
CONV_W = 3
PAGE_SIZE = 128
ROPE_THETA = 10000.0
NORM_EPS = 1e-6
LN_EPS = 1e-5
NEG = -1e30
FORCE = 1e4
OFF_GM = 0
OFF_Q = 512
OFF_KV = 1024
OFF_GATE = 1792
OFF_RW = 1816
IN_COLS = 2840
KV_COLS = 2 * NSA_KV_HEADS * HEAD_DIM
CMP_K = CMP_BLOCK * KV_COLS
GATE_PAD = 128
PACK_COLS = OFF_GATE + RW_COLS + GATE_PAD

LANES = 128
VMEM_LIMIT = 56 * 1024 * 1024


def _cparams(sem):
    return pltpu.CompilerParams(dimension_semantics=sem, vmem_limit_bytes=VMEM_LIMIT)


def _rms(x, g):
    return x * lax.rsqrt(jnp.mean(x * x, axis=-1, keepdims=True) + NORM_EPS) * g


def _dot(a, b):
    return jnp.dot(a.astype(BF16), b.astype(BF16), preferred_element_type=F32)


def _dot_nt(a, b):
    return lax.dot_general(a.astype(BF16), b.astype(BF16), (((1,), (1,)), ((), ())),
                           preferred_element_type=F32)


def _lane_lo(shape):
    return lax.broadcasted_iota(jnp.int32, shape, len(shape) - 1) % LANES < HEAD_DIM


def _head_sum(x):
    cols = []
    for c in range(x.shape[1] // LANES):
        xc = x[:, c * LANES:(c + 1) * LANES]
        lo_mask = _lane_lo(xc.shape)
        lo = jnp.sum(jnp.where(lo_mask, xc, 0.0), axis=-1, keepdims=True)
        hi = jnp.sum(jnp.where(lo_mask, 0.0, xc), axis=-1, keepdims=True)
        cols.append(jnp.where(lo_mask, lo, hi))
    return jnp.concatenate(cols, axis=1)


def _inproj_kernel(x_ref, g_ref, w_ref, cos_ref, sin_ref, ln_ref, ws_ref, bias_ref,
                   ogm_ref, vgm_ref, qpad_ref, kvc_ref, kvs_ref, kvw_ref, kvb_ref, gate_ref, zr_ref, *q_ref,
                   single):
    tm = x_ref.shape[0]
    h = _rms(x_ref[...], g_ref[...])
    z = _dot(h, w_ref[...])
    zg = jax.nn.gelu(z[:, 0:2 * GM_W])
    u = zg[:, :GM_W]
    vv = zg[:, GM_W:]
    mean = jnp.mean(vv, axis=-1, keepdims=True)
    var = jnp.mean(jnp.square(vv - mean), axis=-1, keepdims=True)
    v = (vv - mean) * lax.rsqrt(var + LN_EPS) * ln_ref[0:1, :] + ln_ref[1:2, :]
    vgm_ref[...] = v
    if single:
        ogm_ref[...] = (u * (v * ws_ref[...] + bias_ref[...])).astype(BF16)
    else:
        head_of_lane = lax.broadcasted_iota(jnp.int32, (CHUNK, GM_W), 1) // HEAD_DIM
        causal = (lax.broadcasted_iota(jnp.int32, (CHUNK, CHUNK), 0)
                  >= lax.broadcasted_iota(jnp.int32, (CHUNK, CHUNK), 1))
        wms = [jnp.where(causal, ws_ref[hh], 0.0).astype(BF16) for hh in range(GM_HEADS)]
        for c in range(tm // CHUNK):
            vc = v[c * CHUNK:(c + 1) * CHUNK]
            acc = bias_ref[...]
            for hh in range(GM_HEADS):
                vm = jnp.where(head_of_lane == hh, vc, 0.0).astype(BF16)
                acc = acc + jnp.dot(wms[hh], vm, preferred_element_type=F32)
            ogm_ref[c * CHUNK:(c + 1) * CHUNK, :] = (u[c * CHUNK:(c + 1) * CHUNK] * acc).astype(BF16)
    cos = cos_ref[...]
    sin = sin_ref[...]
    first_half = lax.broadcasted_iota(jnp.int32, (tm, LANES), 1) % HEAD_DIM < HEAD_DIM // 2
    lane_lo = _lane_lo((tm, LANES))

    def rope(xc):
        rot = jnp.where(first_half, pltpu.roll(xc, LANES - HEAD_DIM // 2, 1), pltpu.roll(xc, HEAD_DIM // 2, 1))
        return xc * cos + rot * sin

    scale = HEAD_DIM ** -0.5
    for c in range(NSA_W // LANES):
        qc = rope(z[:, OFF_Q + c * LANES:OFF_Q + (c + 1) * LANES]) * scale
        if single:
            q_ref[0][:, c * LANES:(c + 1) * LANES] = qc
        qr = pltpu.roll(qc, HEAD_DIM, 1)
        if (2 * c) // NSA_REP == 0:
            even = jnp.where(lane_lo, qc, 0.0)
            odd = jnp.where(lane_lo, qr, 0.0)
        else:
            even = jnp.where(lane_lo, 0.0, qr)
            odd = jnp.where(lane_lo, 0.0, qc)
        qpad_ref[:, (2 * c) * LANES:(2 * c + 1) * LANES] = even.astype(BF16)
        qpad_ref[:, (2 * c + 1) * LANES:(2 * c + 2) * LANES] = odd.astype(BF16)
    for j, ref in enumerate((kvc_ref, kvs_ref, kvw_ref)):
        base = OFF_KV + j * KV_COLS
        kk = rope(z[:, base:base + LANES])
        vj = z[:, base + LANES:base + KV_COLS]
        ref[:, 0:LANES] = kk
        ref[:, LANES:KV_COLS] = vj
        if j >= 1:
            kvb_ref[:, (j - 1) * KV_COLS:(j - 1) * KV_COLS + LANES] = kk.astype(BF16)
            kvb_ref[:, (j - 1) * KV_COLS + LANES:j * KV_COLS] = vj.astype(BF16)
    zr_ref[...] = z[:, OFF_GATE:OFF_GATE + RW_COLS]
    gate_ref[...] = jax.nn.sigmoid(z[:, OFF_GATE + RW_COLS:])


def _pack_w_in(w_in):
    gate = w_in[:, OFF_GATE:OFF_RW].reshape(D_MODEL, NSA_HEADS, 3)
    gate = jnp.transpose(gate, (0, 2, 1)).reshape(D_MODEL, 3 * NSA_HEADS)
    gate = jnp.pad(gate, ((0, 0), (0, GATE_PAD - 3 * NSA_HEADS)))
    return jnp.concatenate([w_in[:, :OFF_GATE], w_in[:, OFF_RW:], gate], axis=1).astype(BF16)


def _in_proj(x, g0, w_pack, cos, sin, gm_ln, gm_ws, gm_bs, *, seq_len, tm):
    n = x.shape[0]
    single = seq_len == 1
    if single:
        ws_arg = jnp.repeat(gm_ws[:, 0, 0], HEAD_DIM)[None, :]
        bias_arg = jnp.repeat(gm_bs[:, 0], HEAD_DIM)[None, :]
        ws_spec = pl.BlockSpec((1, GM_W), lambda i: (0, 0))
        bias_spec = pl.BlockSpec((1, GM_W), lambda i: (0, 0))
        tab_map = lambda i: (i, 0)
    else:
        assert seq_len % tm == 0 and tm % CHUNK == 0
        ws_arg = gm_ws
        bias_arg = jnp.repeat(gm_bs.T, HEAD_DIM, axis=1)
        ws_spec = pl.BlockSpec((GM_HEADS, CHUNK, CHUNK), lambda i: (0, 0, 0))
        bias_spec = pl.BlockSpec((CHUNK, GM_W), lambda i: (0, 0))
        tiles = seq_len // tm
        tab_map = lambda i: (i % tiles, 0)
    row = lambda w: pl.BlockSpec((tm, w), lambda i: (i, 0))
    outs = [(GM_W, BF16), (GM_W, F32), (NSA_HEADS * LANES, BF16), (KV_COLS, F32), (KV_COLS, F32),
            (KV_COLS, F32), (2 * KV_COLS, BF16), (GATE_PAD, F32), (RW_COLS, F32)]
    if single:
        outs.append((NSA_W, F32))
    return pl.pallas_call(
        functools.partial(_inproj_kernel, single=single),
        grid=(n // tm,),
        in_specs=[row(D_MODEL), pl.BlockSpec((1, D_MODEL), lambda i: (0, 0)),
                  pl.BlockSpec((D_MODEL, PACK_COLS), lambda i: (0, 0)),
                  pl.BlockSpec((tm, LANES), tab_map), pl.BlockSpec((tm, LANES), tab_map),
                  pl.BlockSpec((2, GM_W), lambda i: (0, 0)), ws_spec, bias_spec],
        out_specs=[row(w) for w, _ in outs],
        out_shape=[jax.ShapeDtypeStruct((n, w), dt) for w, dt in outs],
        compiler_params=_cparams(("parallel",)),
        name="in_proj",
    )(x, g0[None, :], w_pack, cos, sin, gm_ln, ws_arg, bias_arg)


def _rope_tables(pos):
    half = HEAD_DIM // 2
    inv = ROPE_THETA ** (-jnp.arange(half, dtype=F32) / half)
    ang = pos.astype(F32)[:, None] * inv[None, :]
    cos, sin = jnp.cos(ang), jnp.sin(ang)
    return jnp.tile(cos, (1, 4)), jnp.tile(jnp.concatenate([-sin, sin], axis=1), (1, 2))


def _compress_rows(k_rows_ref, v_rows_ref, pe_ref, w_ref, n_out):
    acc_k = jnp.zeros((n_out, LANES), F32)
    acc_v = jnp.zeros((n_out, LANES), F32)
    for l in range(CMP_BLOCK):
        xk = k_rows_ref[pl.ds(l, n_out, stride=CMP_BLOCK), :] + pe_ref[l:l + 1, 0:LANES]
        xv = v_rows_ref[pl.ds(l, n_out, stride=CMP_BLOCK), :] + pe_ref[l:l + 1, LANES:KV_COLS]
        acc_k = acc_k + _dot(xk, w_ref[l, 0:LANES, 0:LANES])
        acc_v = acc_v + _dot(xv, w_ref[l, LANES:KV_COLS, LANES:KV_COLS])
    return jnp.concatenate([acc_k, acc_v], axis=1)


def _compress_kernel(k_ref, v_ref, pe_ref, w_ref, o_ref):
    o_ref[...] = _compress_rows(k_ref, v_ref, pe_ref, w_ref, o_ref.shape[0]).astype(o_ref.dtype)


def _compress_pages_kernel(x_ref, pe_ref, w_ref, o_ref, k_scr, v_scr):
    def transpose_page(p, c):
        rows = pl.ds(pl.multiple_of(p * PAGE_SIZE, PAGE_SIZE), PAGE_SIZE)
        k_scr[rows, :] = x_ref[0, p, 0:LANES, :].T
        v_scr[rows, :] = x_ref[0, p, LANES:KV_COLS, :].T
        return c

    lax.fori_loop(0, x_ref.shape[1], transpose_page, 0)
    o_ref[...] = _compress_rows(k_scr, v_scr, pe_ref, w_ref, o_ref.shape[0])


def _compress_weights(cmp_pe, cmp_w):
    eye = jnp.eye(2 * NSA_KV_HEADS, dtype=F32).reshape(2, NSA_KV_HEADS, 2, NSA_KV_HEADS)
    wb = jnp.einsum('lsde,sgtq->lsgdtqe', cmp_w, eye).reshape(CMP_BLOCK, KV_COLS, KV_COLS).astype(BF16)
    pe = jnp.broadcast_to(cmp_pe[:, :, None, :], (CMP_BLOCK, 2, NSA_KV_HEADS, HEAD_DIM)).reshape(CMP_BLOCK, KV_COLS)
    return pe, wb


def _cmp_const_specs():
    return [pl.BlockSpec((CMP_BLOCK, KV_COLS), lambda i: (0, 0)),
            pl.BlockSpec((CMP_BLOCK, KV_COLS, KV_COLS), lambda i: (0, 0, 0))]


def _compress(x, pe, wb, *, tr):
    n = x.shape[0]
    return pl.pallas_call(
        _compress_kernel,
        grid=(n // tr,),
        in_specs=[pl.BlockSpec((tr, LANES), lambda i: (i, 0)), pl.BlockSpec((tr, LANES), lambda i: (i, 1))]
        + _cmp_const_specs(),
        out_specs=pl.BlockSpec((tr // CMP_BLOCK, KV_COLS), lambda i: (i, 0)),
        out_shape=jax.ShapeDtypeStruct((n // CMP_BLOCK, KV_COLS), BF16),
        compiler_params=_cparams(("parallel",)),
        name="nsa_compress",
    )(x, x, pe, wb)


def _pool_pages(pool):
    depth, n_phys = pool.shape[:2]
    return jnp.transpose(pool, (0, 1, 3, 4, 5, 2)).reshape(depth, n_phys, KV_COLS, PAGE_SIZE)


def _compress_pages(pool_pages, layer, pe, wb, *, pages):
    n_phys = pool_pages.shape[1]
    per_page = PAGE_SIZE // CMP_BLOCK
    return pl.pallas_call(
        _compress_pages_kernel,
        grid=(n_phys // pages,),
        in_specs=[pl.BlockSpec((1, pages, KV_COLS, PAGE_SIZE), lambda i: (layer, i, 0, 0))] + _cmp_const_specs(),
        out_specs=pl.BlockSpec((pages * per_page, KV_COLS), lambda i: (i, 0)),
        out_shape=jax.ShapeDtypeStruct((n_phys * per_page, KV_COLS), F32),
        scratch_shapes=[pltpu.VMEM((pages * PAGE_SIZE, LANES), F32)] * 2,
        compiler_params=_cparams(("parallel",)),
        name="nsa_compress_pages",
    )(pool_pages, pe, wb)


def _group_q(qpad_ref, g):
    return jnp.concatenate([qpad_ref[:, (g * NSA_REP + r) * LANES:(g * NSA_REP + r + 1) * LANES]
                            for r in range(NSA_REP)], axis=0)


def _assemble_heads(pieces, tq):
    lane_lo = _lane_lo((tq, LANES))
    cols = []
    for c in range(NSA_W // LANES):
        g = (2 * c) // NSA_REP
        r = (2 * c) % NSA_REP
        even = pieces[g][r * tq:(r + 1) * tq]
        odd = pieces[g][(r + 1) * tq:(r + 2) * tq]
        if g == 0:
            cols.append(jnp.where(lane_lo, even, pltpu.roll(odd, HEAD_DIM, 1)))
        else:
            cols.append(jnp.where(lane_lo, pltpu.roll(even, HEAD_DIM, 1), odd))
    return cols


def _cmp_attn_kernel(qpad_ref, kvc_ref, oc_ref, sel_ref):
    tq = qpad_ref.shape[0]
    q0 = pl.program_id(1) * tq
    kvc = kvc_ref[0]
    nc = kvc.shape[0]
    ns = nc // 2
    coli = lax.broadcasted_iota(jnp.int32, (1, nc), 1)
    cblk = jnp.where(coli < ns, 2 * coli, 2 * (coli - ns) + 1)
    c_end = cblk * CMP_BLOCK + (CMP_BLOCK - 1)
    rowpos = q0 + lax.broadcasted_iota(jnp.int32, (NSA_REP * tq, 1), 0) % tq
    m_c = c_end <= rowpos
    blk = lax.broadcasted_iota(jnp.int32, (ns, tq), 0)
    cur = (q0 + lax.broadcasted_iota(jnp.int32, (ns, tq), 1)) // SLC_BLOCK
    pieces, imps = [], []
    for g in range(NSA_KV_HEADS):
        qg = _group_q(qpad_ref, g)
        s = _dot_nt(qg, kvc[:, 0:LANES])
        sm = jnp.where(m_c, s, NEG)
        e = jnp.exp(sm - jnp.max(sm, axis=-1, keepdims=True))
        p = jnp.where(m_c, e / jnp.sum(e, axis=-1, keepdims=True), 0.0)
        pieces.append(_dot(p, kvc[:, LANES:KV_COLS]))
        ps = p[0:tq] + p[tq:2 * tq] + p[2 * tq:3 * tq] + p[3 * tq:4 * tq]
        imp = (ps[:, :ns] + ps[:, ns:]).T
        imp = jnp.where((blk == cur) | (blk == 0), FORCE, imp)
        imps.append(jnp.where(blk > cur, NEG, imp))
    works = list(imps)
    sels = [jnp.zeros((ns, tq), F32) for _ in imps]
    for _ in range(min(TOP_N, ns)):
        for g in range(NSA_KV_HEADS):
            mx = jnp.max(works[g], axis=0, keepdims=True)
            first = jnp.min(jnp.where(works[g] == mx, blk, ns), axis=0, keepdims=True)
            pick = blk == first
            sels[g] = jnp.where(pick, 1.0, sels[g])
            works[g] = jnp.where(pick, -jnp.inf, works[g])
    for g in range(NSA_KV_HEADS):
        sel = jnp.where(imps[g] > 0.5 * NEG, sels[g], 0.0)
        sel_ref[:, g * ns:(g + 1) * ns] = sel.T.astype(BF16)
    cols = _assemble_heads(pieces, tq)
    for c in range(NSA_W // LANES):
        oc_ref[:, c * LANES:(c + 1) * LANES] = cols[c]


def _cmp_attn(qpad, kvc_perm, *, batch, seq_len, tq):
    n = qpad.shape[0]
    nq = seq_len // tq
    nc = kvc_perm.shape[1]
    return pl.pallas_call(
        _cmp_attn_kernel,
        grid=(batch, nq),
        in_specs=[pl.BlockSpec((tq, NSA_HEADS * LANES), lambda b, i: (b * nq + i, 0)),
                  pl.BlockSpec((1, nc, KV_COLS), lambda b, i: (b, 0, 0))],
        out_specs=[pl.BlockSpec((tq, NSA_W), lambda b, i: (b * nq + i, 0)),
                   pl.BlockSpec((tq, nc), lambda b, i: (b * nq + i, 0))],
        out_shape=[jax.ShapeDtypeStruct((n, NSA_W), F32), jax.ShapeDtypeStruct((n, nc), BF16)],
        compiler_params=_cparams(("parallel", "parallel")),
        name="nsa_cmp_attn",
    )(qpad, kvc_perm)


SEL_KV_TILE = 512


def _gate_col(gate, br, c, tq):
    lane_lo = _lane_lo((tq, LANES))
    i0 = br * NSA_HEADS + 2 * c
    return jnp.where(lane_lo, gate[:, i0:i0 + 1], gate[:, i0 + 1:i0 + 2])


def _sel_win_kernel(qpad_ref, sel_ref, gate_ref, oc_ref, kvb_ref, out_ref, *, seq_len):
    tq = qpad_ref.shape[0]
    rows = NSA_REP * tq
    kt = min(SEL_KV_TILE, seq_len)
    ns = seq_len // SLC_BLOCK
    q0 = pl.program_id(1) * tq
    assert kt % tq == 0
    j_diag = q0 // kt
    key_blk = lax.broadcasted_iota(jnp.int32, (kt, ns), 0) // SLC_BLOCK
    blk_lane = lax.broadcasted_iota(jnp.int32, (kt, ns), 1)
    lane_kt = lax.broadcasted_iota(jnp.int32, (1, kt), 1)
    n_win = (WINDOW + tq) // LANES
    all_rows = NSA_KV_HEADS * rows
    rowpos = q0 + lax.broadcasted_iota(jnp.int32, (all_rows, 1), 0) % tq
    qs, unsels = [], []
    for g in range(NSA_KV_HEADS):
        unsel = (1.0 - sel_ref[:, g * ns:(g + 1) * ns].astype(F32)).astype(BF16)
        qs.append(_group_q(qpad_ref, g))
        unsels.extend([unsel] * NSA_REP)
    q_all = jnp.concatenate(qs, axis=0)
    lhs = jnp.concatenate([q_all, jnp.concatenate(unsels, axis=0)], axis=1)
    ones = jnp.ones((1, LANES), BF16)

    def with_ones(v):
        return jnp.concatenate([v, jnp.broadcast_to(ones, v.shape)], axis=1)

    def scores(j):
        start = pl.multiple_of(j * kt, kt)
        k = kvb_ref[0, pl.ds(start, kt), 0:LANES]
        penalty = jnp.where(blk_lane == key_blk + j * (kt // SLC_BLOCK), NEG, 0.0).astype(BF16)
        return _dot_nt(lhs, jnp.concatenate([k, penalty], axis=1))

    def absorb(j, s, m, acc):
        v = kvb_ref[0, pl.ds(pl.multiple_of(j * kt, kt), kt), LANES:KV_COLS]
        m_new = jnp.maximum(m, jnp.max(s, axis=-1, keepdims=True))
        p = jnp.exp(s - m_new).astype(BF16)
        return m_new, jnp.exp(m - m_new) * acc + jnp.dot(p, with_ones(v), preferred_element_type=F32)

    def tile(j, carry):
        m, acc, s = carry
        s_next = scores(j + 1)
        m, acc = absorb(j, s, m, acc)
        return m, acc, s_next

    init = (jnp.full((all_rows, 1), NEG, F32), jnp.zeros((all_rows, 2 * LANES), F32), scores(0))
    m, acc, s = lax.fori_loop(0, j_diag, tile, init)
    s = jnp.where(j_diag * kt + lane_kt <= rowpos, s, NEG)
    _, acc = absorb(j_diag, s, m, acc)
    o_sel = acc[:, 0:LANES] / acc[:, LANES:2 * LANES]
    ss, vs = [], []
    for j in range(n_win):
        start_true = q0 - WINDOW + j * LANES
        start = pl.multiple_of(jnp.maximum(start_true, 0), LANES)
        ss.append(_dot_nt(q_all, kvb_ref[0, pl.ds(start, LANES), KV_COLS:KV_COLS + LANES]))
        vs.append(kvb_ref[0, pl.ds(start, LANES), KV_COLS + LANES:2 * KV_COLS])
    kpos = q0 - WINDOW + lax.broadcasted_iota(jnp.int32, (1, n_win * LANES), 1)
    valid = (kpos <= rowpos) & (kpos > rowpos - WINDOW) & (kpos >= 0)
    s = jnp.where(valid, jnp.concatenate(ss, axis=1), NEG)
    e = jnp.where(valid, jnp.exp(s - jnp.max(s, axis=-1, keepdims=True)), 0.0).astype(BF16)
    acc = jnp.dot(e, with_ones(jnp.concatenate(vs, axis=0)), preferred_element_type=F32)
    o_win = acc[:, 0:LANES] / acc[:, LANES:2 * LANES]
    sel_pieces = [o_sel[g * rows:(g + 1) * rows] for g in range(NSA_KV_HEADS)]
    win_pieces = [o_win[g * rows:(g + 1) * rows] for g in range(NSA_KV_HEADS)]
    sel_cols = _assemble_heads(sel_pieces, tq)
    win_cols = _assemble_heads(win_pieces, tq)
    gate = gate_ref[...]
    for c in range(NSA_W // LANES):
        oc = oc_ref[:, c * LANES:(c + 1) * LANES]
        o = (_gate_col(gate, 0, c, tq) * oc + _gate_col(gate, 1, c, tq) * sel_cols[c]
             + _gate_col(gate, 2, c, tq) * win_cols[c])
        out_ref[:, c * LANES:(c + 1) * LANES] = o.astype(BF16)


def _sel_win(qpad, sel, gates, oc, kvb, *, batch, seq_len, tq):
    n = qpad.shape[0]
    nq = seq_len // tq
    ns2 = sel.shape[1]
    row = lambda w: pl.BlockSpec((tq, w), lambda b, i: (b * nq + i, 0))
    return pl.pallas_call(
        functools.partial(_sel_win_kernel, seq_len=seq_len),
        grid=(batch, nq),
        in_specs=[row(NSA_HEADS * LANES), row(ns2), row(GATE_PAD), row(NSA_W),
                  pl.BlockSpec((1, seq_len, 2 * KV_COLS), lambda b, i: (b, 0, 0))],
        out_specs=row(NSA_W),
        out_shape=jax.ShapeDtypeStruct((n, NSA_W), BF16),
        compiler_params=_cparams(("parallel", "parallel")),
        name="nsa_sel_win",
    )(qpad, sel, gates, oc, kvb.reshape(batch, seq_len, 2 * KV_COLS))


def _rwprep_kernel(zr_ref, prev_ref, halo_ref, mu_ref, vec_ref, w2_ref, a2_ref, g2_ref, rk_ref,
                   r_o, w_o, k_o, v_o, a_o, b_o, g_o, bonus_o, *, single, tiles_per_seq):
    tm = zr_ref.shape[0]
    zr = zr_ref[...]
    if single:
        prev = prev_ref[...]
    else:
        first_tile = pl.program_id(0) % tiles_per_seq == 0
        before = jnp.where(first_tile, prev_ref[0], halo_ref[7:8, :])
        rowi = lax.broadcasted_iota(jnp.int32, (tm, 1), 0)
        prev = jnp.where(rowi == 0, before, pltpu.roll(zr, 1, 0))
    zs = zr + (prev - zr) * mu_ref[...]
    r = zs[:, 0:RW_W]
    k = zs[:, RW_W:2 * RW_W]
    v = zs[:, 2 * RW_W:3 * RW_W]
    lora = zs[:, 3 * RW_W:3 * RW_W + W_LORA + A_LORA]
    w0, a0, k_k, k_a = (vec_ref[i:i + 1, :] for i in range(4))
    wlog = -jax.nn.softplus(-(w0 + _dot(jnp.tanh(lora), w2_ref[...]))) - 0.5
    log_decay = -jnp.exp(wlog)
    a = jax.nn.sigmoid(a0 + _dot(lora, a2_ref[...]))
    g = _dot(jax.nn.sigmoid(zs[:, 3 * RW_W + W_LORA + A_LORA:]), g2_ref[...])
    kk = k * k_k
    kk = kk / jnp.maximum(jnp.sqrt(_head_sum(kk * kk)), 1e-12)
    k2 = k * (1.0 + (a - 1.0) * k_a)
    r_o[...] = r
    w_o[...] = jnp.exp(log_decay) if single else log_decay
    k_o[...] = k2
    v_o[...] = v
    a_o[...] = -kk
    b_o[...] = kk * a
    g_o[...] = g
    bonus_o[...] = _head_sum(r * k2 * rk_ref[...]) * v


def _rw_prep(zr, shift_prev, P, *, seq_len, tm):
    n = zr.shape[0]
    single = seq_len == 1
    w2p = jnp.concatenate([P['rw_w2'], jnp.zeros((A_LORA, RW_W), F32)], axis=0).astype(BF16)
    a2p = jnp.concatenate([jnp.zeros((W_LORA, RW_W), F32), P['rw_a2']], axis=0).astype(BF16)
    const = lambda shape: pl.BlockSpec(shape, lambda i: (0,) * len(shape))
    if single:
        prev_arg = shift_prev
        prev_spec = pl.BlockSpec((tm, RW_COLS), lambda i: (i, 0))
        tiles = 1
    else:
        tiles = seq_len // tm
        prev_arg = shift_prev[:, None, :]
        prev_spec = pl.BlockSpec((1, 1, RW_COLS), lambda i: (i // tiles, 0, 0))
    halo_spec = pl.BlockSpec((8, RW_COLS), lambda i: (jnp.maximum(i * (tm // 8) - 1, 0), 0))
    row = pl.BlockSpec((tm, RW_W), lambda i: (i, 0))
    return pl.pallas_call(
        functools.partial(_rwprep_kernel, single=single, tiles_per_seq=tiles),
        grid=(n // tm,),
        in_specs=[pl.BlockSpec((tm, RW_COLS), lambda i: (i, 0)), prev_spec, halo_spec,
                  const((1, RW_COLS)), const((6, RW_W)), const((W_LORA + A_LORA, RW_W)),
                  const((W_LORA + A_LORA, RW_W)), const((G_LORA, RW_W)), const((1, RW_W))],
        out_specs=[row] * 8,
        out_shape=[jax.ShapeDtypeStruct((n, RW_W), F32)] * 8,
        compiler_params=_cparams(("parallel",)),
        name="rwkv_prep",
    )(zr, prev_arg, zr, P['rw_mu'][None, :], P['rw_vec'], w2p, a2p, P['rw_g2'].astype(BF16),
      P['rw_rk'].reshape(1, RW_W))


def _segsum(x, ones2):
    hi = x.astype(BF16)
    lo = (x - hi.astype(F32)).astype(BF16)
    return jnp.dot(jnp.concatenate([hi, lo], axis=1), ones2, preferred_element_type=F32)


def _scan_kernel(r_ref, w_ref, k_ref, v_ref, a_ref, b_ref, s0_ref, y_ref, sfin_ref, s_scr):
    nb, tc = r_ref.shape[0], r_ref.shape[1]
    npair = RW_HEADS // 2
    units = [(b, hp) for b in range(nb) for hp in range(npair)]

    @pl.when(pl.program_id(0) == 0)
    def _():
        s_scr[...] = s0_ref[...]

    ri = lax.broadcasted_iota(jnp.int32, (2 * LANES, LANES), 0) % LANES
    ci = lax.broadcasted_iota(jnp.int32, (2 * LANES, LANES), 1)
    ones2 = (ri // HEAD_DIM == ci // HEAD_DIM).astype(BF16)
    diag = (lax.broadcasted_iota(jnp.int32, (HEAD_DIM, LANES), 0)
            == lax.broadcasted_iota(jnp.int32, (HEAD_DIM, LANES), 1) % HEAD_DIM)

    nu = len(units)
    refs = dict(r=r_ref, w=w_ref, k=k_ref, v=v_ref, a=a_ref, b=b_ref)

    def run(base, nsteps):
        blocks = {name: [ref[b, pl.ds(base, nsteps), hp * LANES:(hp + 1) * LANES] for b, hp in units]
                  for name, ref in refs.items()}
        states = [s_scr[b, hp] for b, hp in units]
        yrows = [[] for _ in units]
        for t in range(nsteps):
            row = lambda name, i: blocks[name][i][t:t + 1, :]
            parts = [states[i] * row('a', i) for i in range(nu)]
            parts += [jnp.where(diag, row('v', i), 0.0) for i in range(nu)]
            red = _segsum(jnp.concatenate(parts, axis=0), ones2)
            for i in range(nu):
                sa = red[i * HEAD_DIM:(i + 1) * HEAD_DIM]
                vb = red[(nu + i) * HEAD_DIM:(nu + i + 1) * HEAD_DIM]
                states[i] = states[i] * row('w', i) + sa * row('b', i) + vb * row('k', i)
            yred = _segsum(jnp.concatenate([states[i] * row('r', i) for i in range(nu)], axis=0), ones2)
            for i in range(nu):
                yb = yred[i * HEAD_DIM:(i + 1) * HEAD_DIM]
                yrows[i].append(jnp.sum(jnp.where(diag, yb, 0.0), axis=0, keepdims=True))
        for i, (b, hp) in enumerate(units):
            s_scr[b, hp] = states[i]
            y_ref[b, pl.ds(base, nsteps), hp * LANES:(hp + 1) * LANES] = jnp.concatenate(yrows[i], axis=0)

    if tc % 8 == 0:
        def group(t8, carry):
            run(pl.multiple_of(t8 * 8, 8), 8)
            return carry

        lax.fori_loop(0, tc // 8, group, 0)
    else:
        run(0, tc)

    @pl.when(pl.program_id(0) == pl.num_programs(0) - 1)
    def _():
        sfin_ref[...] = s_scr[...]


def _state_to_pairs(s):
    nb = s.shape[0]
    return s.reshape(nb, 2, 2, HEAD_DIM, HEAD_DIM).transpose(0, 1, 3, 2, 4).reshape(nb, 2, HEAD_DIM, LANES)


def _pairs_to_state(s):
    nb = s.shape[0]
    return s.reshape(nb, 2, HEAD_DIM, 2, HEAD_DIM).transpose(0, 1, 3, 2, 4).reshape(nb, RW_HEADS, HEAD_DIM, HEAD_DIM)


def _rw_scan(r, w, k, v, a, b, s0, *, batch, seq_len, tc):
    args = [t.reshape(batch, seq_len, RW_W) for t in (r, w, k, v, a, b)]
    blk = pl.BlockSpec((batch, tc, RW_W), lambda c: (0, c, 0))
    st = pl.BlockSpec((batch, 2, HEAD_DIM, LANES), lambda c: (0, 0, 0, 0))
    y, sfin = pl.pallas_call(
        _scan_kernel,
        grid=(seq_len // tc,),
        in_specs=[blk] * 6 + [st],
        out_specs=[blk, st],
        out_shape=[jax.ShapeDtypeStruct((batch, seq_len, RW_W), F32),
                   jax.ShapeDtypeStruct((batch, 2, HEAD_DIM, LANES), F32)],
        scratch_shapes=[pltpu.VMEM((batch, 2, HEAD_DIM, LANES), F32)],
        compiler_params=_cparams(("arbitrary",)),
        name="rwkv_scan",
    )(*args, _state_to_pairs(s0))
    return y.reshape(batch * seq_len, RW_W), _pairs_to_state(sfin)


RW_CHUNK = 64


def _block_rows(x):
    c = x.shape[0]
    tiled = jnp.concatenate([x] * RW_HEADS, axis=0)
    same = (lax.broadcasted_iota(jnp.int32, tiled.shape, 0) // c
            == lax.broadcasted_iota(jnp.int32, tiled.shape, 1) // HEAD_DIM)
    return jnp.where(same, tiled, 0.0)


def _fold_rows(x_bd):
    c = x_bd.shape[0] // RW_HEADS
    return functools.reduce(lambda p, q: p + q, [x_bd[h * c:(h + 1) * c] for h in range(RW_HEADS)])


def _rw_chunk_prep_kernel(r_ref, lw_ref, k_ref, v_ref, a_ref, b_ref,
                          at_o, rt_o, tcat_o, brb_o, uv_o, yv_o, bc_o, kc_o, gend_o):
    c = RW_CHUNK
    n4 = RW_HEADS * c
    rowi = lax.broadcasted_iota(jnp.int32, (c, RW_W), 0)
    ti = lax.broadcasted_iota(jnp.int32, (n4, n4), 0)
    ji = lax.broadcasted_iota(jnp.int32, (n4, n4), 1)
    same_head = ti // c == ji // c
    strict = same_head & (ti % c > ji % c)
    incl = same_head & (ti % c >= ji % c)
    eye = (ti == ji).astype(F32)
    gends, todo = [], []
    for ci in range(r_ref.shape[0] // c):
        rows = slice(ci * c, (ci + 1) * c)
        lw = lw_ref[rows, :]
        cs = lw
        shift = 1
        while shift < c:
            cs = cs + jnp.where(rowi >= shift, pltpu.roll(cs, shift, 0), 0.0)
            shift *= 2
        cs_end = cs[c - 1:c, :]
        gends.append(jnp.exp(cs_end))
        inv = jnp.exp(-cs)
        to_end = jnp.exp(cs_end - cs)
        at = a_ref[rows, :] * jnp.exp(cs - lw)
        rt = r_ref[rows, :] * jnp.exp(cs)
        b, k, v = b_ref[rows, :], k_ref[rows, :], v_ref[rows, :]
        bh4 = jnp.concatenate([b * inv] * RW_HEADS, axis=0)
        kh4 = jnp.concatenate([k * inv] * RW_HEADS, axis=0)
        at_bd, rt_bd, v_bd = _block_rows(at), _block_rows(rt), _block_rows(v)
        a_ab = jnp.where(strict, _dot_nt(at_bd, bh4), 0.0)
        a_ak = jnp.where(strict, _dot_nt(at_bd, kh4), 0.0)
        b_rb = jnp.where(incl, _dot_nt(rt_bd, bh4), 0.0)
        b_rk = jnp.where(incl, _dot_nt(rt_bd, kh4), 0.0)
        at_o[rows, :] = at.astype(BF16)
        rt_o[rows, :] = rt.astype(BF16)
        brb_o[rows, :] = _fold_rows(b_rb).astype(BF16)
        yv_o[rows, :] = _fold_rows(_dot(b_rk, v_bd))
        bc_o[rows, :] = (b * to_end).astype(BF16)
        kc_o[rows, :] = (k * to_end).astype(BF16)
        todo.append((rows, a_ab, _dot(a_ak, v_bd)))
    pair = same_head & ((ti % c) // 2 == (ji % c) // 2)
    t_invs = [eye + jnp.where(pair, a_ab, 0.0) for _, a_ab, _ in todo]
    s = 2
    while s < c:
        lower_left = (((ti % c) // (2 * s) == (ji % c) // (2 * s))
                      & (((ti % c) // s) % 2 == 1) & (((ji % c) // s) % 2 == 0))
        t_invs = [t + _dot(_dot(t, jnp.where(lower_left, a_ab, 0.0)), t)
                  for t, (_, a_ab, _) in zip(t_invs, todo)]
        s *= 2
    for t_inv, (rows, _, akv) in zip(t_invs, todo):
        tcat_o[rows, :] = _fold_rows(t_inv).astype(BF16)
        uv_o[rows, :] = _fold_rows(_dot(t_inv, akv))
    gend_o[...] = jnp.concatenate(gends, axis=0)


def _rw_chain_kernel(at_ref, rt_ref, tcat_ref, brb_ref, uv_ref, yv_ref, bc_ref, kc_ref, v_ref, gend_ref,
                     y_ref, sfin_ref, s_scr):
    nb, tt = at_ref.shape[0], at_ref.shape[1]
    c = RW_CHUNK

    @pl.when(pl.program_id(0) == 0)
    def _():
        s_scr[...] = jnp.zeros_like(s_scr)

    same_head = (lax.broadcasted_iota(jnp.int32, (RW_W, RW_W), 0) // HEAD_DIM
                 == lax.broadcasted_iota(jnp.int32, (RW_W, RW_W), 1) // HEAD_DIM)
    for ci in range(tt // c):
        rows = slice(ci * c, (ci + 1) * c)
        seqs = range(nb)
        states = [s_scr[b] for b in seqs]
        m1s = [_dot_nt(jnp.concatenate([at_ref[b, rows, :], rt_ref[b, rows, :]], axis=0), states[b])
               for b in seqs]
        us = [_dot(tcat_ref[b, rows, :], _block_rows(m1s[b][0:c])) + uv_ref[b, rows, :] for b in seqs]
        for b in seqs:
            y_ref[b, rows, :] = (m1s[b][c:2 * c] + _dot(brb_ref[b, rows, :], _block_rows(us[b]))
                                 + yv_ref[b, rows, :])
        grows = [lax.dot_general(
            jnp.concatenate([us[b], v_ref[b, rows, :]], axis=0).astype(BF16),
            jnp.concatenate([bc_ref[b, rows, :], kc_ref[b, rows, :]], axis=0),
            (((0,), (0,)), ((), ())), preferred_element_type=F32) for b in seqs]
        for b in seqs:
            s_scr[b] = states[b] * gend_ref[b, ci:ci + 1, :] + jnp.where(same_head, grows[b], 0.0)

    @pl.when(pl.program_id(0) == pl.num_programs(0) - 1)
    def _():
        for b in range(nb):
            for h in range(RW_HEADS):
                sfin_ref[b, h] = s_scr[b, h * HEAD_DIM:(h + 1) * HEAD_DIM, h * HEAD_DIM:(h + 1) * HEAD_DIM]


def _rw_chunked(r, lw, k, v, a, b, *, batch, seq_len, tt):
    n = r.shape[0]
    per_tile = tt // RW_CHUNK
    row = pl.BlockSpec((tt, RW_W), lambda i: (i, 0))
    outs = [BF16, BF16, BF16, BF16, F32, F32, BF16, BF16]
    *prep, gend = pl.pallas_call(
        _rw_chunk_prep_kernel,
        grid=(n // tt,),
        in_specs=[row] * 6,
        out_specs=[row] * 8 + [pl.BlockSpec((per_tile, RW_W), lambda i: (i, 0))],
        out_shape=[jax.ShapeDtypeStruct((n, RW_W), dt) for dt in outs]
        + [jax.ShapeDtypeStruct((n // RW_CHUNK, RW_W), F32)],
        compiler_params=_cparams(("parallel",)),
        name="rwkv_chunk_prep",
    )(r, lw, k, v, a, b)
    seq = lambda t: t.reshape(batch, seq_len, RW_W)
    blk = pl.BlockSpec((batch, tt, RW_W), lambda i: (0, i, 0))
    y, s_fin = pl.pallas_call(
        _rw_chain_kernel,
        grid=(seq_len // tt,),
        in_specs=[blk] * 9 + [pl.BlockSpec((batch, per_tile, RW_W), lambda i: (0, i, 0))],
        out_specs=[blk, pl.BlockSpec((batch, RW_HEADS, HEAD_DIM, HEAD_DIM), lambda i: (0, 0, 0, 0))],
        out_shape=[jax.ShapeDtypeStruct((batch, seq_len, RW_W), F32),
                   jax.ShapeDtypeStruct((batch, RW_HEADS, HEAD_DIM, HEAD_DIM), F32)],
        scratch_shapes=[pltpu.VMEM((batch, RW_W, RW_W), F32)],
        compiler_params=_cparams(("arbitrary",)),
        name="rwkv_chain",
    )(*[seq(t) for t in prep], seq(v), gend.reshape(batch, seq_len // RW_CHUNK, RW_W))
    return y.reshape(n, RW_W), s_fin


def _mid_kernel(x_ref, ogm_ref, onsa_ref, y_ref, g_ref, bonus_ref, ng_ref, ln_ref, wo_ref,
                wq_ref, kv_ref, wmo_ref, out_ref):
    tm = x_ref.shape[-2]
    rows = max(tm, 8)
    ld = lambda ref: jnp.broadcast_to(ref[...].reshape(tm, ref.shape[-1]), (rows, ref.shape[-1]))
    x = ld(x_ref)
    y = ld(y_ref)
    ym = _head_sum(y) * (1.0 / HEAD_DIM)
    yc = y - ym
    yv = _head_sum(yc * yc) * (1.0 / HEAD_DIM)
    yn = yc * lax.rsqrt(yv + RW_LN_EPS) * ln_ref[0:1, :] + ln_ref[1:2, :]
    o_rw = (yn + ld(bonus_ref)) * ld(g_ref)
    mix = (_dot(ld(ogm_ref), wo_ref[0:GM_W, :]) + _dot(ld(onsa_ref), wo_ref[GM_W:GM_W + NSA_W, :])
           + _dot(o_rw, wo_ref[GM_W + NSA_W:, :]))
    x = x + _rms(mix, ng_ref[1:2, :])
    q = _dot(_rms(x, ng_ref[2:3, :]), wq_ref[...]) * (MEM_HEAD_DIM ** -0.5)
    kv = kv_ref[0]
    heads = []
    for hh in range(MEM_HEADS):
        lo, hi = hh * MEM_HEAD_DIM, (hh + 1) * MEM_HEAD_DIM
        s = _dot_nt(q[:, lo:hi], kv[:, lo:hi])
        e = jnp.exp(s - jnp.max(s, axis=-1, keepdims=True))
        p = e / jnp.sum(e, axis=-1, keepdims=True)
        heads.append(_dot(p, kv[:, D_MODEL + lo:D_MODEL + hi]))
    o = _dot(jnp.concatenate(heads, axis=1), wmo_ref[...])
    x = x + _rms(o, ng_ref[3:4, :])
    out_ref[...] = x[0:tm].reshape(out_ref.shape)


def _mid(x, ogm, onsa, y, g, bonus, kv_mem, P, *, batch, seq_len, tm):
    n = x.shape[0]
    tiles = seq_len // tm
    if tm >= 8:
        row = lambda w: pl.BlockSpec((tm, w), lambda i: (i, 0))
        shp = lambda t: t
        out_shape = jax.ShapeDtypeStruct((n, D_MODEL), F32)
    else:
        row = lambda w: pl.BlockSpec((1, tm, w), lambda i: (i, 0, 0))
        shp = lambda t: t.reshape(n // tm, tm, t.shape[-1])
        out_shape = jax.ShapeDtypeStruct((n // tm, tm, D_MODEL), F32)
    const = lambda shape: pl.BlockSpec(shape, lambda i: (0,) * len(shape))
    out = pl.pallas_call(
        _mid_kernel,
        grid=(n // tm,),
        in_specs=[row(D_MODEL), row(GM_W), row(NSA_W), row(RW_W), row(RW_W), row(RW_W),
                  const((6, D_MODEL)), const((2, RW_W)), const((D_MODEL, D_MODEL)),
                  const((D_MODEL, D_MODEL)),
                  pl.BlockSpec((1, N_MEM, 2 * D_MODEL), lambda i: (i // tiles, 0, 0)),
                  const((D_MODEL, D_MODEL))],
        out_specs=row(D_MODEL),
        out_shape=out_shape,
        compiler_params=_cparams(("parallel",)),
        name="mix_out_mem_attn",
    )(shp(x), shp(ogm), shp(onsa), shp(y), shp(g), shp(bonus), P['norm_g'], P['rw_vec'][4:6],
      P['w_out'].astype(BF16), P['w_mem_q'].astype(BF16), kv_mem, P['w_mem_o'].astype(BF16))
    return out.reshape(n, D_MODEL)


def _memkv_kernel(x_ref, g_ref, w_ref, o_ref, ob_ref):
    o = _dot(_rms(x_ref[...], g_ref[...]), w_ref[...])
    o_ref[...] = o
    ob_ref[...] = o.astype(BF16)


def _mem_kv(mem, g, w_kv, *, tm):
    n = mem.shape[0]
    return pl.pallas_call(
        _memkv_kernel,
        grid=(n // tm,),
        in_specs=[pl.BlockSpec((tm, D_MODEL), lambda i: (i, 0)), pl.BlockSpec((1, D_MODEL), lambda i: (0, 0)),
                  pl.BlockSpec((D_MODEL, 2 * D_MODEL), lambda i: (0, 0))],
        out_specs=[pl.BlockSpec((tm, 2 * D_MODEL), lambda i: (i, 0))] * 2,
        out_shape=[jax.ShapeDtypeStruct((n, 2 * D_MODEL), F32), jax.ShapeDtypeStruct((n, 2 * D_MODEL), BF16)],
        compiler_params=_cparams(("parallel",)),
        name="mem_kv_proj",
    )(mem, g[None, :], w_kv.astype(BF16))


def _ffn_kernel(x_ref, prev_ref, prev1_ref, ng_ref, wg_ref, wu_ref, cw_ref, cb_ref, wo_ref,
                out_ref, tail_ref, h_scr, acc_scr, tail_scr, *, single, tiles_per_seq):
    tm = x_ref.shape[0]
    i, j = pl.program_id(0), pl.program_id(1)

    @pl.when(j == 0)
    def _():
        h_scr[...] = _rms(x_ref[...], ng_ref[4:5, :]).astype(BF16)
        acc_scr[...] = jnp.zeros_like(acc_scr)

    h = h_scr[...]
    gate = jnp.dot(h, wg_ref[...], preferred_element_type=F32)
    up = jnp.dot(h, wu_ref[...], preferred_element_type=F32)
    if single:
        g2 = prev_ref[...]
        g1 = prev1_ref[...]
        tail_ref[...] = gate
    else:
        first = i % tiles_per_seq == 0
        t0 = jnp.where(first, prev_ref[0, 0:1, :], tail_scr[j, 6:7, :])
        t1 = jnp.where(first, prev_ref[0, 1:2, :], tail_scr[j, 7:8, :])
        rowi = lax.broadcasted_iota(jnp.int32, (tm, 1), 0)
        g1 = jnp.where(rowi == 0, t1, pltpu.roll(gate, 1, 0))
        g2 = jnp.where(rowi == 0, t0, jnp.where(rowi == 1, t1, pltpu.roll(gate, 2, 0)))
        tail_scr[j] = gate[tm - 8:tm]
        tail_ref[0] = gate[tm - 2:tm]
    conv = cb_ref[...] + g2 * cw_ref[0:1, :] + g1 * cw_ref[1:2, :] + gate * cw_ref[2:3, :]
    act = jax.nn.silu(conv) * up
    acc_scr[...] += _dot(act, wo_ref[...])

    @pl.when(j == pl.num_programs(1) - 1)
    def _():
        out_ref[...] = x_ref[...] + _rms(acc_scr[...], ng_ref[5:6, :])


def _ffn(x, conv_prev, P, *, seq_len, tm, tf):
    n = x.shape[0]
    single = seq_len == 1
    nj = D_FF // tf
    w_in = P['ffn_w_in'].astype(BF16)
    if single:
        prev_args = (conv_prev[:, 0], conv_prev[:, 1])
        prev_specs = [pl.BlockSpec((tm, tf), lambda i, j: (i, j))] * 2
        tail_spec = pl.BlockSpec((tm, tf), lambda i, j: (i, j))
        tail_shape = jax.ShapeDtypeStruct((n, D_FF), F32)
        tiles = 1
    else:
        tiles = seq_len // tm
        prev_args = (conv_prev, conv_prev)
        prev_specs = [pl.BlockSpec((1, CONV_W - 1, tf), lambda i, j: (i // tiles, 0, j))] * 2
        tail_spec = pl.BlockSpec((1, CONV_W - 1, tf), lambda i, j: (i, 0, j))
        tail_shape = jax.ShapeDtypeStruct((n // tm, CONV_W - 1, D_FF), F32)
    out, tail = pl.pallas_call(
        functools.partial(_ffn_kernel, single=single, tiles_per_seq=tiles),
        grid=(n // tm, nj),
        in_specs=[pl.BlockSpec((tm, D_MODEL), lambda i, j: (i, 0)), *prev_specs,
                  pl.BlockSpec((6, D_MODEL), lambda i, j: (0, 0)),
                  pl.BlockSpec((D_MODEL, tf), lambda i, j: (0, j)),
                  pl.BlockSpec((D_MODEL, tf), lambda i, j: (0, nj + j)),
                  pl.BlockSpec((CONV_W, tf), lambda i, j: (0, j)),
                  pl.BlockSpec((1, tf), lambda i, j: (0, j)),
                  pl.BlockSpec((tf, D_MODEL), lambda i, j: (j, 0))],
        out_specs=[pl.BlockSpec((tm, D_MODEL), lambda i, j: (i, 0)), tail_spec],
        out_shape=[jax.ShapeDtypeStruct((n, D_MODEL), F32), tail_shape],
        scratch_shapes=[pltpu.VMEM((tm, D_MODEL), BF16), pltpu.VMEM((tm, D_MODEL), F32),
                        pltpu.VMEM((nj, 8, tf), F32)],
        compiler_params=_cparams(("arbitrary", "arbitrary")),
        name="conv_ffn",
    )(x, *prev_args, P['norm_g'], w_in, w_in, P['ffn_conv_w'], P['ffn_conv_b'][None, :],
      P['ffn_w_out'].astype(BF16))
    if single:
        tail = jnp.stack([conv_prev[:, 1], tail], axis=1)
    else:
        tail = tail[tiles - 1::tiles]
    return out, tail


def _sample_cmp_win_kernel(pt_ref, qpad_ref, kvc_hbm, win_ref, kvw_ref, oc_ref, ow_ref, idx_ref,
                           kbuf, sem):
    b = pl.program_id(0)
    n_pages = kbuf.shape[0]
    per_page = PAGE_SIZE // CMP_BLOCK

    def page_copy(p):
        return pltpu.make_async_copy(kvc_hbm.at[pl.ds(pt_ref[b, p], 1)], kbuf.at[pl.ds(p, 1)], sem)

    def start(p, c):
        page_copy(p).start()
        return c

    def wait(p, c):
        page_copy(p).wait()
        return c

    lax.fori_loop(0, n_pages, start, 0)
    lax.fori_loop(0, n_pages, wait, 0)
    q8 = jnp.concatenate([qpad_ref[0, :, h * LANES:(h + 1) * LANES].astype(F32) for h in range(NSA_HEADS)],
                         axis=0)
    ss = [_dot_nt(q8, kbuf[:, j * KV_COLS:j * KV_COLS + LANES]) for j in range(per_page)]
    mx = functools.reduce(jnp.maximum, [jnp.max(s, axis=-1, keepdims=True) for s in ss])
    es = [jnp.exp(s - mx) for s in ss]
    den = functools.reduce(lambda x, y: x + y, [jnp.sum(e, axis=-1, keepdims=True) for e in es])
    ps = [e / den for e in es]
    oc = functools.reduce(lambda x, y: x + y,
                          [_dot(ps[j], kbuf[:, j * KV_COLS + LANES:(j + 1) * KV_COLS]) for j in range(per_page)])
    group0 = lax.broadcasted_iota(jnp.int32, (NSA_HEADS, HEAD_DIM), 0) < NSA_REP
    own_half = lambda o: jnp.where(group0, o[:, 0:HEAD_DIM], o[:, HEAD_DIM:LANES])
    oc_ref[0] = own_half(oc)
    lane = lax.broadcasted_iota(jnp.int32, (NSA_KV_HEADS, n_pages), 1)
    ns_past = 2 * n_pages
    vals = []
    for jj in range(2):
        pj = ps[2 * jj] + ps[2 * jj + 1]
        rows = [jnp.sum(pj[g * NSA_REP:(g + 1) * NSA_REP], axis=0, keepdims=True) for g in range(NSA_KV_HEADS)]
        val = jnp.concatenate(rows, axis=0)
        n_of = 2 * lane + jj
        vals.append(jnp.where(n_of == 0, FORCE, val))
    cur_val = jnp.full((NSA_KV_HEADS, 1), FORCE, F32)
    big = ns_past + 1
    out_lane = lax.broadcasted_iota(jnp.int32, (NSA_KV_HEADS, LANES), 1)
    idx = jnp.zeros((NSA_KV_HEADS, LANES), jnp.int32)
    for rnd in range(TOP_N):
        mx = jnp.maximum(jnp.maximum(jnp.max(vals[0], axis=-1, keepdims=True),
                                     jnp.max(vals[1], axis=-1, keepdims=True)), cur_val)
        c0 = jnp.min(jnp.where(vals[0] == mx, 2 * lane, big), axis=-1, keepdims=True)
        c1 = jnp.min(jnp.where(vals[1] == mx, 2 * lane + 1, big), axis=-1, keepdims=True)
        c2 = jnp.where(cur_val == mx, ns_past, big)
        pick = jnp.minimum(jnp.minimum(c0, c1), c2)
        idx = jnp.where(out_lane == rnd, pick, idx)
        vals[0] = jnp.where(2 * lane == pick, -jnp.inf, vals[0])
        vals[1] = jnp.where(2 * lane + 1 == pick, -jnp.inf, vals[1])
        cur_val = jnp.where(pick == ns_past, -jnp.inf, cur_val)
    idx_ref[0] = jnp.concatenate([idx, jnp.zeros((8 - NSA_KV_HEADS, LANES), jnp.int32)], axis=0)
    wb = win_ref.shape[1]
    s_old = _dot_nt(q8, win_ref[0, :, 0:LANES])
    new_k = jnp.broadcast_to(kvw_ref[0, :, 0:LANES], (8, LANES))
    s_new = _dot_nt(q8, new_k)[:, 0:1]
    keep = lax.broadcasted_iota(jnp.int32, (1, wb), 1) > wb - WINDOW
    s_old = jnp.where(keep, s_old, NEG)
    mx = jnp.maximum(jnp.max(s_old, axis=-1, keepdims=True), s_new)
    e_old = jnp.exp(s_old - mx)
    e_new = jnp.exp(s_new - mx)
    den = jnp.sum(e_old, axis=-1, keepdims=True) + e_new
    new_v = kvw_ref[0, :, LANES:KV_COLS].astype(BF16).astype(F32)
    ow_ref[0] = own_half(_dot(e_old / den, win_ref[0, :, LANES:KV_COLS])
                         + (e_new / den).astype(BF16).astype(F32) * new_v)


def _sample_cmp_win(page_table, qpad, kvc_phys, win_buf, kvw):
    nb, n_pages = page_table.shape
    wb = win_buf.shape[1]
    row_w = (PAGE_SIZE // CMP_BLOCK) * KV_COLS
    piece = jax.ShapeDtypeStruct((nb, NSA_HEADS, HEAD_DIM), F32)
    grid_spec = pltpu.PrefetchScalarGridSpec(
        num_scalar_prefetch=1,
        grid=(nb,),
        in_specs=[pl.BlockSpec((1, 1, NSA_HEADS * LANES), lambda b, pt: (b, 0, 0)),
                  pl.BlockSpec(memory_space=pl.ANY),
                  pl.BlockSpec((1, wb, KV_COLS), lambda b, pt: (b, 0, 0)),
                  pl.BlockSpec((1, 1, KV_COLS), lambda b, pt: (b, 0, 0))],
        out_specs=[pl.BlockSpec((1, NSA_HEADS, HEAD_DIM), lambda b, pt: (b, 0, 0))] * 2
        + [pl.BlockSpec((1, 8, LANES), lambda b, pt: (b, 0, 0))],
        scratch_shapes=[pltpu.VMEM((n_pages, row_w), F32), pltpu.SemaphoreType.DMA(())],
    )
    return pl.pallas_call(
        _sample_cmp_win_kernel,
        grid_spec=grid_spec,
        out_shape=[piece, piece, jax.ShapeDtypeStruct((nb, 8, LANES), jnp.int32)],
        compiler_params=_cparams(("arbitrary",)),
        name="nsa_sample_cmp_win",
    )(page_table, qpad.reshape(nb, 1, NSA_HEADS * LANES), kvc_phys.reshape(-1, row_w), win_buf,
      kvw.reshape(nb, 1, KV_COLS))


def _sample_sel_kernel(pt_ref, idx_ref, q_ref, *refs, n_pages):
    blk_refs = refs[:TOP_N]
    kvs_ref, gate_ref, oc_ref, ow_ref, out_ref = refs[TOP_N:]
    b, g = pl.program_id(0), pl.program_id(1)
    q8 = jnp.concatenate([q_ref[0, 0], jnp.zeros((8 - NSA_REP, HEAD_DIM), F32)], axis=0)
    kv_new = kvs_ref[0]
    k_new = jnp.where(g == 0, kv_new[0:1], kv_new[1:2])
    v_new = jnp.where(g == 0, kv_new[2:3], kv_new[3:4])
    s_new = _dot_nt(q8, jnp.broadcast_to(k_new, (8, HEAD_DIM)))[:, 0:1]
    lane = lax.broadcasted_iota(jnp.int32, (8, PAGE_SIZE), 1)
    is_cur, ss = [], []
    for n in range(TOP_N):
        blk = idx_ref[b, g * TOP_N + n]
        is_cur.append(blk >= 2 * n_pages)
        half = blk % (PAGE_SIZE // SLC_BLOCK)
        s_page = _dot(q8, blk_refs[n][0, 0, 0, 0])
        ss.append(jnp.where(is_cur[n], jnp.where(lane == 0, s_new, NEG),
                            jnp.where(lane // SLC_BLOCK == half, s_page, NEG)))
    m = functools.reduce(jnp.maximum, [jnp.max(s, axis=-1, keepdims=True) for s in ss])
    ps = [jnp.exp(s - m) for s in ss]
    den = functools.reduce(lambda x, y: x + y, [jnp.sum(p, axis=-1, keepdims=True) for p in ps])
    v_new_r = v_new.astype(BF16).astype(F32)
    acc = jnp.zeros((8, HEAD_DIM), F32)
    for n in range(TOP_N):
        pv_new = ps[n][:, 0:1].astype(BF16).astype(F32) * v_new_r
        acc = acc + jnp.where(is_cur[n], pv_new, _dot_nt(ps[n], blk_refs[n][0, 0, 1, 0]))
    o_s = (acc / den)[0:NSA_REP]
    head = g * NSA_REP + lax.broadcasted_iota(jnp.int32, (NSA_REP, GATE_PAD), 0)
    lane_g = lax.broadcasted_iota(jnp.int32, (NSA_REP, GATE_PAD), 1)
    gate = jnp.broadcast_to(gate_ref[0], (NSA_REP, GATE_PAD))
    gcol = lambda br: jnp.sum(jnp.where(lane_g == br * NSA_HEADS + head, gate, 0.0), axis=-1, keepdims=True)
    out_ref[0, 0] = gcol(0) * oc_ref[0, 0] + gcol(1) * o_s + gcol(2) * ow_ref[0, 0]


def _sample_sel(page_table, idx, q, pool_pages, layer, kvs, gates, oc, ow):
    nb, n_pages = page_table.shape
    halves = PAGE_SIZE // SLC_BLOCK

    def page_spec(n):
        def blk_map(b, g, pt, ix):
            page = jnp.minimum(ix[b, g * TOP_N + n] // halves, n_pages - 1)
            return (layer, pt[b, page], 0, g, 0, 0)
        return pl.BlockSpec((1, 1, 2, 1, HEAD_DIM, PAGE_SIZE), blk_map)

    per_bg = pl.BlockSpec((1, 1, NSA_REP, HEAD_DIM), lambda b, g, pt, ix: (b, g, 0, 0))
    per_b = lambda shape: pl.BlockSpec((1,) + shape, lambda b, g, pt, ix: (b, 0, 0))
    grid_spec = pltpu.PrefetchScalarGridSpec(
        num_scalar_prefetch=2,
        grid=(nb, NSA_KV_HEADS),
        in_specs=[per_bg] + [page_spec(n) for n in range(TOP_N)]
        + [per_b((2 * NSA_KV_HEADS, HEAD_DIM)), per_b((1, GATE_PAD)), per_bg, per_bg],
        out_specs=per_bg,
    )
    grouped = lambda t: t.reshape(nb, NSA_KV_HEADS, NSA_REP, HEAD_DIM)
    out = pl.pallas_call(
        functools.partial(_sample_sel_kernel, n_pages=n_pages),
        grid_spec=grid_spec,
        out_shape=jax.ShapeDtypeStruct((nb, NSA_KV_HEADS, NSA_REP, HEAD_DIM), F32),
        compiler_params=_cparams(("arbitrary", "arbitrary")),
        name="nsa_sample_sel",
    )(page_table, idx, grouped(q), *([pool_pages] * TOP_N), kvs.reshape(nb, 2 * NSA_KV_HEADS, HEAD_DIM),
      gates.reshape(nb, 1, GATE_PAD), grouped(oc), grouped(ow))
    return out.reshape(nb, NSA_W)


def _tile(n, want):
    t = min(n, want)
    while n % t or (t % 8 and t != n):
        t -= 1
    return t


def _prompt_layer(x, tabs, mem_prompt, P, *, batch, seq_len):
    kv_f32, kv_b16 = _mem_kv(mem_prompt, P['mem_g'], P['w_mem_kv'], tm=_tile(mem_prompt.shape[0], 256))
    w_pack = _pack_w_in(P['w_in'])
    ogm, _, qpad, kvc, kvs, kvw, kvb, gates, zr = _in_proj(
        x, P['norm_g'][0], w_pack, tabs[0], tabs[1], P['gm_ln'], P['gm_ws'], P['gm_bs'],
        seq_len=seq_len, tm=_tile(seq_len, 256))
    pe, wb = _compress_weights(P['cmp_pe'], P['cmp_w'])
    nc = seq_len // CMP_BLOCK
    comp = _compress(kvc, pe, wb, tr=seq_len)
    comp = comp.reshape(batch, nc // 2, 2, KV_COLS).transpose(0, 2, 1, 3).reshape(batch, nc, KV_COLS)
    oc, sel = _cmp_attn(qpad, comp, batch=batch, seq_len=seq_len, tq=_tile(seq_len, 256))
    onsa = _sel_win(qpad, sel, gates, oc, kvb, batch=batch, seq_len=seq_len, tq=128)
    shift0 = jnp.zeros((batch, RW_COLS), F32)
    r, lw, k, v, a, b, g, bonus = _rw_prep(zr, shift0, P, seq_len=seq_len, tm=_tile(seq_len, 512))
    y, s_fin = _rw_chunked(r, lw, k, v, a, b, batch=batch, seq_len=seq_len, tt=_tile(seq_len, 512))
    x = _mid(x, ogm, onsa, y, g, bonus, kv_b16.reshape(batch, N_MEM, 2 * D_MODEL), P,
             batch=batch, seq_len=seq_len, tm=_tile(seq_len, 256))
    conv0 = jnp.zeros((batch, CONV_W - 1, D_FF), F32)
    x, conv_new = _ffn(x, conv0, P, seq_len=seq_len, tm=_tile(seq_len, 512), tf=D_FF // 2)
    kvshape = (batch, seq_len, 2, NSA_KV_HEADS, HEAD_DIM)
    wbp = min(WINDOW, seq_len)
    states = dict(
        cmp=kvc.reshape(kvshape), slc=kvs.reshape(kvshape), win=kvw.reshape(kvshape)[:, seq_len - wbp:],
        rw=s_fin, shift=zr.reshape(batch, seq_len, RW_COLS)[:, -1], conv=conv_new,
        mem=kv_f32.reshape(batch, N_MEM, 2, MEM_HEADS, MEM_HEAD_DIM))
    return x, states


def _sample_layer(x, tabs, page_table, cmp_pages, slc_pages, layer, win_buf, mem_kv, rw_state, shift_prev,
                  conv_prev, P):
    nb = x.shape[0]
    w_pack = _pack_w_in(P['w_in'])
    ogm, vgm, qpad, kvc, kvs, kvw, _, gates, zr, q = _in_proj(
        x, P['norm_g'][0], w_pack, tabs[0], tabs[1], P['gm_ln'], P['gm_ws'], P['gm_bs'], seq_len=1, tm=nb)
    pe, wb = _compress_weights(P['cmp_pe'], P['cmp_w'])
    n_phys = cmp_pages.shape[1]
    kvc_phys = _compress_pages(cmp_pages, layer, pe, wb, pages=_tile(n_phys, 64))
    oc, ow, idx = _sample_cmp_win(page_table, qpad, kvc_phys, win_buf.reshape(nb, -1, KV_COLS), kvw)
    idx = idx[:, :NSA_KV_HEADS, :TOP_N].reshape(nb, NSA_KV_HEADS * TOP_N)
    slc6 = slc_pages.reshape(slc_pages.shape[0], n_phys, 2, NSA_KV_HEADS, HEAD_DIM, PAGE_SIZE)
    onsa = _sample_sel(page_table, idx, q, slc6, layer, kvs, gates, oc, ow)
    r, w, k, v, a, b, g, bonus = _rw_prep(zr, shift_prev, P, seq_len=1, tm=nb)
    y, s_fin = _rw_scan(r, w, k, v, a, b, rw_state, batch=nb, seq_len=1, tc=1)
    x = _mid(x, ogm, onsa, y, g, bonus, mem_kv.reshape(nb, N_MEM, 2 * D_MODEL), P, batch=nb, seq_len=1, tm=1)
    x, conv_new = _ffn(x, conv_prev, P, seq_len=1, tm=nb, tf=256)
    kvshape = (nb, 1, 2, NSA_KV_HEADS, HEAD_DIM)
    states = dict(cmp=kvc.reshape(kvshape), slc=kvs.reshape(kvshape), win=kvw.reshape(kvshape), rw=s_fin,
                  shift=zr, conv=conv_new, gv=vgm.reshape(nb, 1, GM_W))
    return x, states


def kernel(x_prompt, x_sample, cache_cmp_kv, cache_slc_kv, cache_win_kv, cache_mem_kv, state_rwkv, state_rwkv_shift, state_ffn_conv, page_table, mem_prompt, norm_g, w_in, w_out, gm_ln, gm_ws, gm_bs, cmp_pe, cmp_w, rw_mu, rw_vec, rw_w2, rw_a2, rw_g2, rw_rk, mem_g, w_mem_q, w_mem_kv, w_mem_o, ffn_w_in, ffn_conv_w, ffn_conv_b, ffn_w_out):
    bp, tp = x_prompt.shape[:2]
    nb, ts = x_sample.shape[:2]
    assert ts == 1 and tp % 512 == 0
    depth = norm_g.shape[0]
    past = page_table.shape[1] * PAGE_SIZE
    tabs_p = _rope_tables(jnp.arange(tp))
    tabs_s = _rope_tables(jnp.full((nb,), past))
    xp = x_prompt.reshape(bp * tp, D_MODEL)
    xs = x_sample.reshape(nb, D_MODEL)
    mem = mem_prompt.reshape(bp * N_MEM, D_MODEL)
    weights = dict(norm_g=norm_g, w_in=w_in, w_out=w_out, gm_ln=gm_ln, gm_ws=gm_ws, gm_bs=gm_bs,
                   cmp_pe=cmp_pe, cmp_w=cmp_w, rw_mu=rw_mu, rw_vec=rw_vec, rw_w2=rw_w2, rw_a2=rw_a2,
                   rw_g2=rw_g2, rw_rk=rw_rk, mem_g=mem_g, w_mem_q=w_mem_q, w_mem_kv=w_mem_kv,
                   w_mem_o=w_mem_o, ffn_w_in=ffn_w_in, ffn_conv_w=ffn_conv_w, ffn_conv_b=ffn_conv_b,
                   ffn_w_out=ffn_w_out)
    cmp_pages = _pool_pages(cache_cmp_kv)
    slc_pages = _pool_pages(cache_slc_kv)
    ps, ss = [], []
    for l in range(depth):
        P = {name: val[l] for name, val in weights.items()}
        xp, st = _prompt_layer(xp, tabs_p, mem, P, batch=bp, seq_len=tp)
        ps.append(st)
        xs, st = _sample_layer(xs, tabs_s, page_table, cmp_pages, slc_pages, l, cache_win_kv[l],
                               cache_mem_kv[l], state_rwkv[l], state_rwkv_shift[l], state_ffn_conv[l], P)
        ss.append(st)
    stack = lambda sts, name: jnp.stack([st[name] for st in sts])
    return (xp.reshape(bp, tp, D_MODEL), xs.reshape(nb, 1, D_MODEL),
            stack(ps, 'cmp'), stack(ps, 'slc'), stack(ps, 'win'), stack(ps, 'rw'), stack(ps, 'shift'),
            stack(ps, 'conv'), stack(ps, 'mem'),
            stack(ss, 'cmp'), stack(ss, 'slc'), stack(ss, 'win'), stack(ss, 'rw'), stack(ss, 'shift'),
            stack(ss, 'conv'), stack(ss, 'gv'))
```
